```python
import math
import jax
import jax.numpy as jnp
from jax import lax
import numpy as np

D_MODEL = 1024
BATCH = 2
SEQ = 16384
DEPTH = 1

DIFF_HEADS = 4
DIFF_HD = 64
DIFF_VD = 2 * DIFF_HD
DIL_PATTERNS = ((128, 1), (512, 4), (2048, 16))
N_DIL_GROUPS = len(DIL_PATTERNS)
DIL_HEADS = 4
DIL_HD = 64
REL_BUCKETS = 32
REL_MAX_DIST = 2048
N_ATTN_HEADS = DIFF_HEADS + N_DIL_GROUPS * DIL_HEADS
PEER_HEADS = 8
PEER_NKEYS = 128
PEER_TOPK = 16
PEER_DKEY = 128
N_EXPERTS = PEER_NKEYS * PEER_NKEYS
Q_BLOCK = 128
TOKEN_BLOCK = 128
ALPHA = (2.0 * DEPTH) ** 0.25
BETA = (8.0 * DEPTH) ** -0.25
LN_EPS = 1e-5
SUBLN_EPS = 1e-5

DIFF_QK_W = DIFF_HEADS * 2 * DIFF_HD
DIFF_V_W = DIFF_HEADS * DIFF_VD
DIL_W = N_DIL_GROUPS * DIL_HEADS * DIL_HD
DIL_OUT_W = DIL_HEADS * DIL_HD
OFF_DQ = 0
OFF_DK = OFF_DQ + DIFF_QK_W
OFF_DV = OFF_DK + DIFF_QK_W
OFF_LQ = OFF_DV + DIFF_V_W
OFF_LK = OFF_LQ + DIL_W
OFF_LV = OFF_LK + DIL_W
OFF_GA = OFF_LV + DIL_W
OFF_GB = OFF_GA + D_MODEL
IN_COLS = OFF_GB + D_MODEL

kernel_name = 'hybrid_diffattn_dilated_peer_deepnorm'


def layer_norm(x, g, b):
    xf = x.astype(jnp.float32)
    mu = jnp.mean(xf, axis=-1, keepdims=True)
    var = jnp.mean(jnp.square(xf - mu), axis=-1, keepdims=True)
    return ((xf - mu) * lax.rsqrt(var + LN_EPS) * g + b).astype(x.dtype)


def rel_bucket(dist):
    n = jnp.maximum(dist, 0)
    max_exact = REL_BUCKETS // 2
    nf = jnp.maximum(n, 1).astype(jnp.float32)
    large = max_exact + (jnp.log(nf / max_exact) / math.log(REL_MAX_DIST / max_exact)
                         * (REL_BUCKETS - max_exact)).astype(jnp.int32)
    large = jnp.minimum(large, REL_BUCKETS - 1)
    return jnp.where(n < max_exact, n, large)


def diff_attention(q, k, v, lam, lam_init, bias_dist, subln_g):
    B, S, H = q.shape[0], q.shape[1], q.shape[2]
    nb = S // Q_BLOCK
    qb = jnp.moveaxis(q.reshape(B, nb, Q_BLOCK, H, 2, DIFF_HD), 1, 0)
    kpos = jnp.arange(S)
    scale = DIFF_HD ** -0.5

    def block(args):
        qi, i = args
        qpos = i * Q_BLOCK + jnp.arange(Q_BLOCK)
        dist = qpos[:, None] - kpos[None, :]
        bias = jnp.moveaxis(jnp.take(bias_dist, jnp.clip(dist, 0, S - 1), axis=0), -1, 0)
        logits = jnp.einsum('bqhcd,bkhcd->bchqk', qi, k).astype(jnp.float32) * scale
        logits = logits + bias.astype(jnp.float32)[None, None]
        logits = jnp.where(dist >= 0, logits, -jnp.inf)
        p = jax.nn.softmax(logits, axis=-1)
        attn = p[:, 0] - lam * p[:, 1]
        return jnp.einsum('bhqk,bkhe->bqhe', attn.astype(v.dtype), v)

    out = lax.map(block, (qb, jnp.arange(nb)))
    out = jnp.moveaxis(out, 0, 1).reshape(B, S, H, DIFF_VD).astype(jnp.float32)
    out = out * lax.rsqrt(jnp.mean(jnp.square(out), axis=-1, keepdims=True) + SUBLN_EPS)
    out = out * subln_g * (1.0 - lam_init)
    return out.reshape(B, S, H * DIFF_VD)


def dilated_group(q, k, v, window, dilation, bias_h):
    B, S, H, dh = q.shape
    n_off = window // dilation + 1
    offs = jnp.arange(n_off) * dilation
    bias = jnp.take(bias_h, rel_bucket(offs), axis=0).astype(jnp.float32)
    kp = jnp.pad(k, ((0, 0), (window, 0), (0, 0), (0, 0)))
    vp = jnp.pad(v, ((0, 0), (window, 0), (0, 0), (0, 0)))
    nb = S // Q_BLOCK
    qb = jnp.moveaxis(q.reshape(B, nb, Q_BLOCK, H, dh), 1, 0)
    scale = dh ** -0.5

    def block(args):
        qi, i = args
        qpos = i * Q_BLOCK + jnp.arange(Q_BLOCK)
        kpos = qpos[:, None] - offs[None, :]
        kg = jnp.take(kp, kpos + window, axis=1)
        vg = jnp.take(vp, kpos + window, axis=1)
        logits = jnp.einsum('bqhd,bqjhd->bhqj', qi, kg).astype(jnp.float32) * scale
        logits = logits + bias.T[None, :, None, :]
        logits = jnp.where(kpos >= 0, logits, -jnp.inf)
        lse = jax.nn.logsumexp(logits, axis=-1)
        p = jnp.exp(logits - lse[..., None])
        o = jnp.einsum('bhqj,bqjhd->bqhd', p, vg.astype(jnp.float32))
        return o, lse

    o, lse = lax.map(block, (qb, jnp.arange(nb)))
    o = jnp.moveaxis(o, 0, 1).reshape(B, S, H, dh)
    lse = jnp.transpose(lse, (1, 0, 3, 2)).reshape(B, S, H)
    return o, lse


def dilated_attention(q, k, v, rel_bias_dil):
    B, S = q.shape[0], q.shape[1]
    outs, lses = [], []
    for g, (window, dilation) in enumerate(DIL_PATTERNS):
        o, lse = dilated_group(q[:, :, g], k[:, :, g], v[:, :, g], window, dilation,
                               rel_bias_dil[:, g * DIL_HEADS:(g + 1) * DIL_HEADS])
        outs.append(o)
        lses.append(lse)
    o_all = jnp.stack(outs, axis=2)
    w = jax.nn.softmax(jnp.stack(lses, axis=2), axis=2)
    out = jnp.sum(w[..., None] * o_all, axis=2)
    return out.reshape(B, S, DIL_OUT_W)


def peer(x, w_q, sub_keys, u, v):
    B, S, D = x.shape
    T = B * S
    xt = x.reshape(T // TOKEN_BLOCK, TOKEN_BLOCK, D)

    def block(xb):
        qb = (xb @ w_q).reshape(TOKEN_BLOCK, PEER_HEADS, 2, PEER_DKEY)
        s = jnp.einsum('thcd,hcnd->thcn', qb, sub_keys).astype(jnp.float32)
        s_top, i_top = lax.top_k(s, PEER_TOPK)
        cand = s_top[:, :, 0, :, None] + s_top[:, :, 1, None, :]
        cand_idx = i_top[:, :, 0, :, None] * PEER_NKEYS + i_top[:, :, 1, None, :]
        cand = cand.reshape(TOKEN_BLOCK, PEER_HEADS, PEER_TOPK * PEER_TOPK)
        cand_idx = cand_idx.reshape(TOKEN_BLOCK, PEER_HEADS, PEER_TOPK * PEER_TOPK)
        best, pos = lax.top_k(cand, PEER_TOPK)
        eidx = jnp.take_along_axis(cand_idx, pos, axis=-1)
        g = jax.nn.softmax(best, axis=-1)
        ug = jnp.take(u, eidx, axis=0)
        vg = jnp.take(v, eidx, axis=0)
        h = jax.nn.gelu(jnp.einsum('td,thkd->thk', xb, ug).astype(jnp.float32), approximate=False)
        return jnp.einsum('thk,thkd->td', (g * h).astype(x.dtype), vg)

    return lax.map(block, xt).reshape(B, S, D)


def setup_inputs(seed: int = 0) -> dict:
    key = jax.random.key(seed)
    ks = jax.random.split(key, 16)
    f32 = jnp.float32
    L = DEPTH
    x = jax.random.normal(ks[0], (BATCH, SEQ, D_MODEL), f32)
    col_scale = jnp.ones((IN_COLS,), f32).at[OFF_DV:OFF_LQ].set(BETA).at[OFF_LV:OFF_GA].set(BETA)
    w_in = jax.random.normal(ks[1], (L, D_MODEL, IN_COLS), f32) * D_MODEL ** -0.5 * col_scale
    diff_lambda = jax.random.normal(ks[2], (L, 4, DIFF_HD), f32) * 0.1
    diff_subln_g = 1.0 + 0.02 * jax.random.normal(ks[3], (L, DIFF_VD), f32)
    w_branch_diff = jax.random.normal(ks[4], (L, DIFF_V_W, D_MODEL), f32) * DIFF_V_W ** -0.5
    w_branch_dil = jax.random.normal(ks[5], (L, DIL_OUT_W, D_MODEL), f32) * DIL_OUT_W ** -0.5
    w_out = jax.random.normal(ks[6], (L, D_MODEL, D_MODEL), f32) * D_MODEL ** -0.5 * BETA
    ln1_g = 1.0 + 0.02 * jax.random.normal(ks[7], (L, D_MODEL), f32)
    ln1_b = 0.02 * jax.random.normal(ks[8], (L, D_MODEL), f32)
    peer_wq = jax.random.normal(ks[9], (L, D_MODEL, PEER_HEADS * 2 * PEER_DKEY), f32) * D_MODEL ** -0.5
    peer_keys = jax.random.normal(ks[10], (L, PEER_HEADS, 2, PEER_NKEYS, PEER_DKEY), f32) * PEER_DKEY ** -0.5
    peer_u = jax.random.normal(ks[11], (L, N_EXPERTS, D_MODEL), f32) * D_MODEL ** -0.5
    peer_v = jax.random.normal(ks[12], (L, N_EXPERTS, D_MODEL), f32) * BETA
    ln2_g = 1.0 + 0.02 * jax.random.normal(ks[13], (L, D_MODEL), f32)
    ln2_b = 0.02 * jax.random.normal(ks[14], (L, D_MODEL), f32)
    rel_bias = 0.2 * jax.random.normal(ks[15], (REL_BUCKETS, N_ATTN_HEADS), f32)
    return {'x': x, 'w_in': w_in, 'diff_lambda': diff_lambda, 'diff_subln_g': diff_subln_g,
            'w_branch_diff': w_branch_diff, 'w_branch_dil': w_branch_dil, 'w_out': w_out,
            'ln1_g': ln1_g, 'ln1_b': ln1_b, 'peer_wq': peer_wq, 'peer_keys': peer_keys,
            'peer_u': peer_u, 'peer_v': peer_v, 'ln2_g': ln2_g, 'ln2_b': ln2_b,
            'rel_bias': rel_bias}


def reference(x, w_in, diff_lambda, diff_subln_g, w_branch_diff, w_branch_dil, w_out,
              ln1_g, ln1_b, peer_wq, peer_keys, peer_u, peer_v, ln2_g, ln2_b, rel_bias):
    B, S, D = x.shape
    bias_dist = jnp.take(rel_bias, rel_bucket(jnp.arange(S)), axis=0)
    for l in range(DEPTH):
        lam_init = 0.8 - 0.6 * math.exp(-0.3 * l)
        lp = diff_lambda[l].astype(jnp.float32)
        lam = jnp.exp(jnp.sum(lp[0] * lp[1])) - jnp.exp(jnp.sum(lp[2] * lp[3])) + lam_init

        h = x @ w_in[l]
        dq = h[..., OFF_DQ:OFF_DK].reshape(B, S, DIFF_HEADS, 2, DIFF_HD)
        dk = h[..., OFF_DK:OFF_DV].reshape(B, S, DIFF_HEADS, 2, DIFF_HD)
        dv = h[..., OFF_DV:OFF_LQ].reshape(B, S, DIFF_HEADS, DIFF_VD)
        lq = h[..., OFF_LQ:OFF_LK].reshape(B, S, N_DIL_GROUPS, DIL_HEADS, DIL_HD)
        lk = h[..., OFF_LK:OFF_LV].reshape(B, S, N_DIL_GROUPS, DIL_HEADS, DIL_HD)
        lv = h[..., OFF_LV:OFF_GA].reshape(B, S, N_DIL_GROUPS, DIL_HEADS, DIL_HD)
        gate_a = jax.nn.sigmoid(h[..., OFF_GA:OFF_GB])
        gate_b = jax.nn.sigmoid(h[..., OFF_GB:IN_COLS])

        y_a = diff_attention(dq, dk, dv, lam, lam_init, bias_dist[:, :DIFF_HEADS],
                             diff_subln_g[l]).astype(x.dtype)
        y_b = dilated_attention(lq, lk, lv, rel_bias[:, DIFF_HEADS:]).astype(x.dtype)
        mixed = (gate_a * (y_a @ w_branch_diff[l]) + gate_b * (y_b @ w_branch_dil[l])) @ w_out[l]
        x = layer_norm(ALPHA * x + mixed, ln1_g[l], ln1_b[l])

        ffn = peer(x, peer_wq[l], peer_keys[l], peer_u[l], peer_v[l])
        x = layer_norm(ALPHA * x + ffn, ln2_g[l], ln2_b[l])
    return x
```

```python
import functools
import math

import numpy as np
import jax
import jax.numpy as jnp
from jax import lax
from jax.experimental import pallas as pl
from jax.experimental.pallas import tpu as pltpu

F32 = jnp.float32
BF16 = jnp.bfloat16
I32 = jnp.int32

DEPTH = 1
DIFF_HEADS = 4
DIFF_HD = 64
DIFF_VD = 2 * DIFF_HD
DIL_PATTERNS = ((128, 1), (512, 4), (2048, 16))
N_DIL_GROUPS = len(DIL_PATTERNS)
DIL_HEADS = 4
DIL_HD = 64
REL_BUCKETS = 32
REL_MAX_DIST = 2048
PEER_HEADS = 8
PEER_NKEYS = 128
PEER_TOPK = 16
PEER_DKEY = 128
ALPHA = (2.0 * DEPTH) ** 0.25
LN_EPS = 1e-5
SUBLN_EPS = 1e-5

DIFF_QK_W = DIFF_HEADS * 2 * DIFF_HD
DIFF_V_W = DIFF_HEADS * DIFF_VD
DIL_W = N_DIL_GROUPS * DIL_HEADS * DIL_HD
DIL_GROUP_W = DIL_HEADS * DIL_HD

LANES = 128
SUBLANES = 8
VMEM_LIMIT = 56 * 1024 * 1024

MASK_VALUE = -1e30

PEER_PAIRS = PEER_HEADS * PEER_TOPK
HI16 = -65536


def _cparams(sem):
    return pltpu.CompilerParams(dimension_semantics=sem, vmem_limit_bytes=VMEM_LIMIT)


def _rel_bucket(dist):
    n = jnp.maximum(dist, 0)
    max_exact = REL_BUCKETS // 2
    nf = jnp.maximum(n, 1).astype(F32)
    large = max_exact + (jnp.log(nf / max_exact) / math.log(REL_MAX_DIST / max_exact)
                         * (REL_BUCKETS - max_exact)).astype(I32)
    large = jnp.minimum(large, REL_BUCKETS - 1)
    return jnp.where(n < max_exact, n, large)


def _proj_kernel(x_ref, w_ref, *out_refs, pieces):
    xb = x_ref[...].astype(BF16)
    for o_ref, (lo, hi, scale) in zip(out_refs, pieces):
        acc = jnp.dot(xb, w_ref[:, lo:hi], preferred_element_type=F32)
        if scale != 1.0:
            acc = acc * scale
        o_ref[...] = acc.astype(BF16)


def _input_projection(x2d, w_bf16, tm=256):
    T, D = x2d.shape
    widths = (DIFF_QK_W, DIFF_QK_W, DIFF_V_W, DIL_W, DIL_W, DIL_W, D, D)
    scales = (DIFF_HD ** -0.5, 1.0, 1.0, DIL_HD ** -0.5, 1.0, 1.0, 1.0, 1.0)
    pieces, off = [], 0
    for w, s in zip(widths, scales):
        pieces.append((off, off + w, s))
        off += w
    assert off == w_bf16.shape[1]
    return pl.pallas_call(
        functools.partial(_proj_kernel, pieces=tuple(pieces)),
        grid=(T // tm,),
        in_specs=[pl.BlockSpec((tm, D), lambda i: (i, 0)),
                  pl.BlockSpec(w_bf16.shape, lambda i: (0, 0))],
        out_specs=[pl.BlockSpec((tm, w), lambda i: (i, 0)) for w in widths],
        out_shape=[jax.ShapeDtypeStruct((T, w), BF16) for w in widths],
        compiler_params=_cparams(("parallel",)),
        name="input_projection",
    )(x2d, w_bf16)


def _diff_kernel(lam_ref, q_ref, k_ref, v_ref, bias_ref, g_ref, o_ref, m_ref, l_ref, acc_ref,
                 *, tq, tk, n_near, out_scale):
    i = pl.program_id(2)
    q = q_ref[...]
    lane = lax.broadcasted_iota(I32, q.shape, 1)
    zero = jnp.zeros_like(q)
    qq = jnp.concatenate([jnp.where(lane < DIFF_HD, q, zero), jnp.where(lane >= DIFF_HD, q, zero)], axis=0)

    m_ref[...] = jnp.full(m_ref.shape, MASK_VALUE, F32)
    l_ref[...] = jnp.zeros(l_ref.shape, F32)
    acc_ref[...] = jnp.zeros(acc_ref.shape, F32)

    def step(j, bias):
        start = pl.multiple_of(j * tk, tk)
        k = k_ref[pl.ds(start, tk), :]
        v = v_ref[pl.ds(start, tk), :]
        s = lax.dot_general(qq, k, (((1,), (1,)), ((), ())), preferred_element_type=F32)
        if bias is not None:
            s = (s.reshape(2, tq, tk) + bias[None]).reshape(2 * tq, tk)
        m_prev = m_ref[...]
        m_new = jnp.maximum(m_prev, jnp.max(s, axis=-1, keepdims=True))
        alpha = jnp.exp(m_prev - m_new)
        p = jnp.exp(s - m_new)
        l_ref[...] = alpha * l_ref[...] + jnp.sum(p, axis=-1, keepdims=True)
        acc_ref[...] = alpha * acc_ref[...] + jnp.dot(p.astype(BF16), v, preferred_element_type=F32)
        m_ref[...] = m_new

    def far_body(j, carry):
        step(j, None)
        return carry

    lax.fori_loop(0, jnp.maximum(i - (n_near - 1), 0), far_body, 0)

    for delta in range(n_near):
        @pl.when(i >= delta)
        def _():
            step(i - delta, bias_ref[0, delta])

    o = acc_ref[...] / l_ref[...]
    out = o[:tq] - lam_ref[0] * o[tq:]
    out = out * lax.rsqrt(jnp.mean(out * out, axis=-1, keepdims=True) + SUBLN_EPS)
    out = out * g_ref[...] * out_scale
    o_ref[...] = out.astype(o_ref.dtype)


def _near_tile_count(t):
    n = np.arange(1, 4 * REL_MAX_DIST, dtype=np.float32)
    large = 16 + (np.log(n / 16) / math.log(REL_MAX_DIST / 16) * 16).astype(np.int32)
    first_const = int(np.max(np.nonzero(large < REL_BUCKETS - 1)[0])) + 2
    first_const += 16
    delta = 0
    while (delta - 1) * t + 1 < first_const:
        delta += 1
    return delta


def _diff_attention(dq, dk, dv, lam, lam_init, bias_dist, subln_g, B, S, tq=512):
    T = B * S
    tk = tq
    nq = S // tq
    n_near = min(_near_tile_count(tq), nq)
    r = jnp.arange(tq)[:, None]
    c = jnp.arange(tk)[None, :]
    far = bias_dist[S - 1] if n_near < nq else jnp.zeros_like(bias_dist[0])
    tiles = []
    for delta in range(n_near):
        dist = delta * tq + r - c
        val = jnp.take(bias_dist, jnp.clip(dist, 0, S - 1), axis=0) - far
        val = jnp.where((dist >= 0)[..., None], val, MASK_VALUE)
        tiles.append(jnp.moveaxis(val, -1, 0))
    bias_tiles = jnp.stack(tiles, axis=1).astype(F32)

    kern = functools.partial(_diff_kernel, tq=tq, tk=tk, n_near=n_near, out_scale=1.0 - lam_init)
    return pl.pallas_call(
        kern,
        grid=(B, DIFF_HEADS, nq),
        in_specs=[pl.BlockSpec(memory_space=pltpu.SMEM),
                  pl.BlockSpec((tq, LANES), lambda b, h, i: (b * nq + i, h)),
                  pl.BlockSpec((S, LANES), lambda b, h, i: (b, h)),
                  pl.BlockSpec((S, LANES), lambda b, h, i: (b, h)),
                  pl.BlockSpec((1, n_near, tq, tk), lambda b, h, i: (h, 0, 0, 0)),
                  pl.BlockSpec((1, DIFF_VD), lambda b, h, i: (0, 0))],
        out_specs=pl.BlockSpec((tq, LANES), lambda b, h, i: (b * nq + i, h)),
        out_shape=jax.ShapeDtypeStruct((T, DIFF_V_W), BF16),
        scratch_shapes=[pltpu.VMEM((2 * tq, 1), F32), pltpu.VMEM((2 * tq, 1), F32),
                        pltpu.VMEM((2 * tq, DIFF_VD), F32)],
        compiler_params=_cparams(("parallel", "parallel", "parallel")),
        name="diff_attention",
    )(lam.reshape(1), dq, dk, dv, bias_tiles, subln_g.reshape(1, DIFF_VD))


def _dil_blocks(t):
    return tuple((g, delta) for g, (window, _) in enumerate(DIL_PATTERNS) for delta in range(window // t + 1))


def _dil_kernel(*refs, tq, blocks):
    nb = len(blocks)
    q_ref = refs[0]
    k_refs = refs[1:1 + nb]
    v_refs = refs[1 + nb:1 + 2 * nb]
    bias_ref = refs[1 + 2 * nb]
    o_ref = refs[2 + 2 * nb]
    m_ref, l_ref, acc_ref = refs[3 + 2 * nb:]
    i = pl.program_id(1)

    lane = lax.broadcasted_iota(I32, (tq, DIL_GROUP_W), 1)
    head_masks = [(lane >= h * DIL_HD) & (lane < (h + 1) * DIL_HD) for h in range(DIL_HEADS)]

    m_ref[...] = jnp.full(m_ref.shape, MASK_VALUE, F32)
    l_ref[...] = jnp.zeros(l_ref.shape, F32)
    acc_ref[...] = jnp.zeros(acc_ref.shape, F32)

    def step(qs, bi):
        k = k_refs[bi][...]
        v = v_refs[bi][...]
        s = lax.dot_general(qs, k, (((1,), (1,)), ((), ())), preferred_element_type=F32) + bias_ref[bi]
        m_prev = m_ref[...]
        m_new = jnp.maximum(m_prev, jnp.max(s, axis=-1, keepdims=True))
        alpha = jnp.exp(m_prev - m_new)
        p = jnp.exp(s - m_new)
        l_ref[...] = alpha * l_ref[...] + jnp.sum(p, axis=-1, keepdims=True)
        acc_ref[...] = alpha * acc_ref[...] + jnp.dot(p.astype(BF16), v, preferred_element_type=F32)
        m_ref[...] = m_new

    for g in range(N_DIL_GROUPS):
        q = q_ref[:, g * DIL_GROUP_W:(g + 1) * DIL_GROUP_W]
        zero = jnp.zeros_like(q)
        qs = jnp.concatenate([jnp.where(hm, q, zero) for hm in head_masks], axis=0)
        for bi, (bg, delta) in enumerate(blocks):
            if bg != g:
                continue
            if delta == 0:
                step(qs, bi)
            else:
                @pl.when(i >= delta)
                def _():
                    step(qs, bi)

    o = acc_ref[...] / l_ref[...]
    y = jnp.zeros((tq, DIL_GROUP_W), F32)
    for h, hm in enumerate(head_masks):
        y = y + jnp.where(hm, o[h * tq:(h + 1) * tq], 0.0)
    o_ref[...] = y.astype(o_ref.dtype)


def _dilated_attention(lq, lk, lv, rel_bias_dil, B, S, t=128):
    T = B * S
    nq = S // t
    blocks = _dil_blocks(t)
    max_w = max(w for w, _ in DIL_PATTERNS)
    bd = jnp.take(rel_bias_dil, _rel_bucket(jnp.arange(max_w + 1)), axis=0).astype(F32)
    r = jnp.arange(t)[:, None]
    c = jnp.arange(t)[None, :]
    tiles = []
    for g, delta in blocks:
        window, dil = DIL_PATTERNS[g]
        dist = delta * t + r - c
        valid = (dist >= 0) & (dist <= window) & (dist % dil == 0)
        val = jnp.take(bd[:, g * DIL_HEADS:(g + 1) * DIL_HEADS], jnp.clip(dist, 0, max_w), axis=0)
        val = jnp.where(valid[..., None], val, MASK_VALUE)
        tiles.append(jnp.moveaxis(val, -1, 0).reshape(DIL_HEADS * t, t))
    bias_tiles = jnp.stack(tiles, axis=0)
    nb = len(blocks)

    def kv_spec(g, delta):
        return pl.BlockSpec((t, DIL_GROUP_W), lambda b, i: (b * nq + jnp.maximum(i - delta, 0), g))

    kv_specs = [kv_spec(g, delta) for g, delta in blocks]
    return pl.pallas_call(
        functools.partial(_dil_kernel, tq=t, blocks=blocks),
        grid=(B, nq),
        in_specs=[pl.BlockSpec((t, DIL_W), lambda b, i: (b * nq + i, 0))] + kv_specs + kv_specs
                 + [pl.BlockSpec((nb, DIL_HEADS * t, t), lambda b, i: (0, 0, 0))],
        out_specs=pl.BlockSpec((t, DIL_GROUP_W), lambda b, i: (b * nq + i, 0)),
        out_shape=jax.ShapeDtypeStruct((T, DIL_GROUP_W), BF16),
        scratch_shapes=[pltpu.VMEM((DIL_HEADS * t, 1), F32), pltpu.VMEM((DIL_HEADS * t, 1), F32),
                        pltpu.VMEM((DIL_HEADS * t, DIL_GROUP_W), F32)],
        compiler_params=_cparams(("parallel", "parallel")),
        name="dilated_attention",
    )(lq, *([lk] * nb), *([lv] * nb), bias_tiles)


def _layer_norm(y, g, b):
    mu = jnp.mean(y, axis=-1, keepdims=True)
    var = jnp.mean(jnp.square(y - mu), axis=-1, keepdims=True)
    return (y - mu) * lax.rsqrt(var + LN_EPS) * g + b


def _post_kernel(x_ref, ya_ref, yb_ref, ga_ref, gb_ref, wd_ref, wl_ref, wo_ref, g1_ref, b1_ref, wq_ref,
                 x1_ref, pq_ref):
    a = jnp.dot(ya_ref[...], wd_ref[...], preferred_element_type=F32)
    b = jnp.dot(yb_ref[...], wl_ref[...], preferred_element_type=F32)
    mix = jax.nn.sigmoid(ga_ref[...].astype(F32)) * a + jax.nn.sigmoid(gb_ref[...].astype(F32)) * b
    mixed = jnp.dot(mix.astype(BF16), wo_ref[...], preferred_element_type=F32)
    x1 = _layer_norm(ALPHA * x_ref[...] + mixed, g1_ref[...], b1_ref[...])
    x1_ref[...] = x1
    pq_ref[...] = jnp.dot(x1.astype(BF16), wq_ref[...], preferred_element_type=F32).astype(BF16)


def _post_attention(x2d, ya, yb, ga, gb, wd, wl, wo, g1, b1, wq, tm=256):
    T, D = x2d.shape
    NQ = wq.shape[1]

    def row(w):
        return pl.BlockSpec((tm, w), lambda i: (i, 0))

    def full(a):
        return pl.BlockSpec(a.shape, lambda i: (0,) * a.ndim)

    return pl.pallas_call(
        _post_kernel,
        grid=(T // tm,),
        in_specs=[row(D), row(ya.shape[1]), row(yb.shape[1]), row(D), row(D),
                  full(wd), full(wl), full(wo), full(g1), full(b1), full(wq)],
        out_specs=[row(D), row(NQ)],
        out_shape=[jax.ShapeDtypeStruct((T, D), F32), jax.ShapeDtypeStruct((T, NQ), BF16)],
        compiler_params=_cparams(("parallel",)),
        name="post_attention",
    )(x2d, ya, yb, ga, gb, wd, wl, wo, g1, b1, wq)


def _top_rows(s, k):
    n = s.shape[0]
    row = lax.broadcasted_iota(I32, s.shape, 0)
    vals, idxs = [], []
    for _ in range(k):
        m = jnp.max(s, axis=0, keepdims=True)
        idx = jnp.min(jnp.where(s == m, row, n), axis=0, keepdims=True)
        s = jnp.where(row == idx, -jnp.inf, s)
        vals.append(m)
        idxs.append(idx)
    return jnp.concatenate(vals, axis=0), jnp.concatenate(idxs, axis=0)


def _retrieve_kernel(pq_ref, keys_ref, e_ref, gt_ref):
    tt = pq_ref.shape[0]
    e_rows = []
    for h in range(PEER_HEADS):
        tops = []
        for c in range(2):
            col = (h * 2 + c) * PEER_DKEY
            qh = pq_ref[:, col:col + PEER_DKEY]
            s = lax.dot_general(keys_ref[h, c], qh, (((1,), (1,)), ((), ())),
                                preferred_element_type=F32)
            tops.append(_top_rows(s, PEER_TOPK))
        (s0, i0), (s1, i1) = tops
        cand = jnp.concatenate([s0[a:a + 1] + s1 for a in range(PEER_TOPK)], axis=0)
        cand_idx = jnp.concatenate([i0[a:a + 1] * PEER_NKEYS + i1 for a in range(PEER_TOPK)], axis=0)
        best, pos = _top_rows(cand, PEER_TOPK)
        row = lax.broadcasted_iota(I32, cand.shape, 0)
        eidx = jnp.concatenate(
            [jnp.max(jnp.where(row == pos[a:a + 1], cand_idx, -1), axis=0, keepdims=True)
             for a in range(PEER_TOPK)], axis=0)
        e = jnp.exp(best - jnp.max(best, axis=0, keepdims=True))
        gt_ref[h * PEER_TOPK:(h + 1) * PEER_TOPK, :] = e / jnp.sum(e, axis=0, keepdims=True)
        e_rows.append(eidx)
    e_ref[...] = jnp.concatenate(e_rows, axis=0).T


def _peer_retrieve(pq, keys_bf16, tt=128):
    T = pq.shape[0]
    return pl.pallas_call(
        _retrieve_kernel,
        grid=(T // tt,),
        in_specs=[pl.BlockSpec((tt, pq.shape[1]), lambda i: (i, 0)),
                  pl.BlockSpec(keys_bf16.shape, lambda i: (0, 0, 0, 0))],
        out_specs=[pl.BlockSpec((tt, PEER_PAIRS), lambda i: (i, 0)),
                   pl.BlockSpec((PEER_PAIRS, tt), lambda i: (0, i))],
        out_shape=[jax.ShapeDtypeStruct((T, PEER_PAIRS), I32), jax.ShapeDtypeStruct((PEER_PAIRS, T), F32)],
        compiler_params=_cparams(("parallel",)),
        name="peer_retrieve",
    )(pq, keys_bf16)


def _pack_table(w):
    n, d = w.shape
    bits = lax.bitcast_convert_type(w.astype(BF16), jnp.uint16).astype(jnp.uint32)
    bits = bits.reshape(n // 2, 2, d)
    packed = bits[:, 0] | (bits[:, 1] << 16)
    return lax.bitcast_convert_type(packed, I32).reshape(n // 2, d // LANES, LANES)


def _expert_tile(tab_ref, e):
    tile = tab_ref[e >> 1]
    shift = ((e & 1) ^ 1) << 4
    return pltpu.bitcast((tile << shift) & HI16, F32)


def _peer_u_kernel(e_ref, x_ref, tab_ref, ht_ref, prod_ref, part_ref):
    tt = x_ref.shape[0]
    lane = lax.broadcasted_iota(I32, (PEER_PAIRS, tt), 1)
    ht_ref[...] = jnp.zeros(ht_ref.shape, F32)

    def token(t, carry):
        x = x_ref[t]
        for g in range(PEER_PAIRS // SUBLANES):
            for j in range(SUBLANES):
                e = e_ref[t, g * SUBLANES + j]
                prod_ref[j * SUBLANES:(j + 1) * SUBLANES, :] = _expert_tile(tab_ref, e) * x
            r = prod_ref[pl.ds(0, SUBLANES, stride=SUBLANES), :]
            for cc in range(1, SUBLANES):
                r = r + prod_ref[pl.ds(cc, SUBLANES, stride=SUBLANES), :]
            part_ref[g * SUBLANES:(g + 1) * SUBLANES, :] = r
        col = jnp.sum(part_ref[...], axis=1, keepdims=True)
        ht_ref[...] = jnp.where(lane == t, col, ht_ref[...])
        return carry

    lax.fori_loop(0, tt, token, 0)


def _peer_u(eidx, x1_tiles, utab, tt=128):
    T = eidx.shape[0]
    return pl.pallas_call(
        _peer_u_kernel,
        grid=(T // tt,),
        in_specs=[pl.BlockSpec((tt, PEER_PAIRS), lambda i: (i, 0), memory_space=pltpu.SMEM),
                  pl.BlockSpec((tt, SUBLANES, LANES), lambda i: (i, 0, 0)),
                  pl.BlockSpec(memory_space=pltpu.VMEM)],
        out_specs=pl.BlockSpec((PEER_PAIRS, tt), lambda i: (0, i)),
        out_shape=jax.ShapeDtypeStruct((PEER_PAIRS, T), F32),
        scratch_shapes=[pltpu.VMEM((SUBLANES * SUBLANES, LANES), F32), pltpu.VMEM((PEER_PAIRS, LANES), F32)],
        compiler_params=_cparams(("arbitrary",)),
        name="peer_u",
    )(eidx, x1_tiles, utab)


def _peer_w_kernel(gt_ref, ht_ref, w_ref):
    h = ht_ref[...]
    gelu = 0.5 * h * (1.0 + lax.erf(h * (2.0 ** -0.5)))
    w_ref[...] = (gt_ref[...] * gelu).T


def _peer_weights(gt, ht, tt=512):
    P, T = gt.shape
    return pl.pallas_call(
        _peer_w_kernel,
        grid=(T // tt,),
        in_specs=[pl.BlockSpec((P, tt), lambda i: (0, i)), pl.BlockSpec((P, tt), lambda i: (0, i))],
        out_specs=pl.BlockSpec((tt, P), lambda i: (i, 0)),
        out_shape=jax.ShapeDtypeStruct((T, P), F32),
        compiler_params=_cparams(("parallel",)),
        name="peer_weights",
    )(gt, ht)


def _peer_v_kernel(e_ref, w_ref, tab_ref, o_ref):
    tt = o_ref.shape[0]
    n_acc = 4

    def token(t, carry):
        accs = [jnp.zeros((SUBLANES, LANES), F32) for _ in range(n_acc)]
        for k in range(PEER_PAIRS):
            accs[k % n_acc] = accs[k % n_acc] + w_ref[t, k] * _expert_tile(tab_ref, e_ref[t, k])
        o_ref[t] = (accs[0] + accs[1]) + (accs[2] + accs[3])
        return carry

    lax.fori_loop(0, tt, token, 0)


def _peer_v(eidx, w, vtab, tt=128):
    T = eidx.shape[0]
    return pl.pallas_call(
        _peer_v_kernel,
        grid=(T // tt,),
        in_specs=[pl.BlockSpec((tt, PEER_PAIRS), lambda i: (i, 0), memory_space=pltpu.SMEM),
                  pl.BlockSpec((tt, PEER_PAIRS), lambda i: (i, 0), memory_space=pltpu.SMEM),
                  pl.BlockSpec(memory_space=pltpu.VMEM)],
        out_specs=pl.BlockSpec((tt, SUBLANES, LANES), lambda i: (i, 0, 0)),
        out_shape=jax.ShapeDtypeStruct((T, SUBLANES, LANES), F32),
        compiler_params=_cparams(("arbitrary",)),
        name="peer_v",
    )(eidx, w, vtab)


def _final_kernel(x1_ref, f_ref, g_ref, b_ref, o_ref):
    o_ref[...] = _layer_norm(ALPHA * x1_ref[...] + f_ref[...], g_ref[...], b_ref[...])


def _final_norm(x1, ffn, g, b, tm=512):
    T, D = x1.shape
    return pl.pallas_call(
        _final_kernel,
        grid=(T // tm,),
        in_specs=[pl.BlockSpec((tm, D), lambda i: (i, 0)), pl.BlockSpec((tm, D), lambda i: (i, 0)),
                  pl.BlockSpec((1, D), lambda i: (0, 0)), pl.BlockSpec((1, D), lambda i: (0, 0))],
        out_specs=pl.BlockSpec((tm, D), lambda i: (i, 0)),
        out_shape=jax.ShapeDtypeStruct((T, D), F32),
        compiler_params=_cparams(("parallel",)),
        name="final_norm",
    )(x1, ffn, g, b)


def kernel(x, w_in, diff_lambda, diff_subln_g, w_branch_diff, w_branch_dil, w_out, ln1_g, ln1_b,
           peer_wq, peer_keys, peer_u, peer_v, ln2_g, ln2_b, rel_bias):
    B, S, D = x.shape
    T = B * S
    assert w_in.shape[0] == DEPTH
    bias_dist = jnp.take(rel_bias, _rel_bucket(jnp.arange(S)), axis=0)
    x2d = x.reshape(T, D)
    for l in range(DEPTH):
        lam_init = 0.8 - 0.6 * math.exp(-0.3 * l)
        lp = diff_lambda[l].astype(F32)
        lam = jnp.exp(jnp.sum(lp[0] * lp[1])) - jnp.exp(jnp.sum(lp[2] * lp[3])) + lam_init

        dq, dk, dv, lq, lk, lv, ga, gb = _input_projection(x2d, w_in[l].astype(BF16))
        ya = _diff_attention(dq, dk, dv, lam, lam_init, bias_dist[:, :DIFF_HEADS], diff_subln_g[l], B, S)
        yb = _dilated_attention(lq, lk, lv, rel_bias[:, DIFF_HEADS:], B, S)
        x1, pq = _post_attention(
            x2d, ya, yb, ga, gb, w_branch_diff[l].astype(BF16), w_branch_dil[l].astype(BF16),
            w_out[l].astype(BF16), ln1_g[l].reshape(1, D), ln1_b[l].reshape(1, D), peer_wq[l].astype(BF16))

        eidx, gt = _peer_retrieve(pq, peer_keys[l].astype(BF16))
        ht = _peer_u(eidx, x1.reshape(T, D // LANES, LANES), _pack_table(peer_u[l]))
        w = _peer_weights(gt, ht)
        ffn = _peer_v(eidx, w, _pack_table(peer_v[l]))
        x2d = _final_norm(x1, ffn.reshape(T, D), ln2_g[l].reshape(1, D), ln2_b[l].reshape(1, D))
    return x2d.reshape(B, S, D)
```

```python
import functools
import math

import numpy as np
import jax
import jax.numpy as jnp
from jax import lax
from jax.experimental import pallas as pl
from jax.experimental.pallas import tpu as pltpu

F32 = jnp.float32
BF16 = jnp.bfloat16
I32 = jnp.int32

DEPTH = 1
DIFF_HEADS = 4
DIFF_HD = 64
DIFF_VD = 2 * DIFF_HD
DIL_PATTERNS = ((128, 1), (512, 4), (2048, 16))
N_DIL_GROUPS = len(DIL_PATTERNS)
DIL_HEADS = 4
DIL_HD = 64
REL_BUCKETS = 32
REL_MAX_DIST = 2048
PEER_HEADS = 8
PEER_NKEYS = 128
PEER_TOPK = 16
PEER_DKEY = 128
ALPHA = (2.0 * DEPTH) ** 0.25
LN_EPS = 1e-5
SUBLN_EPS = 1e-5

DIFF_QK_W = DIFF_HEADS * 2 * DIFF_HD
DIFF_V_W = DIFF_HEADS * DIFF_VD
DIL_W = N_DIL_GROUPS * DIL_HEADS * DIL_HD
DIL_GROUP_W = DIL_HEADS * DIL_HD

LANES = 128
SUBLANES = 8
VMEM_LIMIT = 56 * 1024 * 1024

MASK_VALUE = -1e30

PEER_PAIRS = PEER_HEADS * PEER_TOPK
HI16 = -65536
LOG2_E = math.log2(math.e)


def _cparams(sem):
    return pltpu.CompilerParams(dimension_semantics=sem, vmem_limit_bytes=VMEM_LIMIT)


def _rel_bucket(dist):
    n = jnp.maximum(dist, 0)
    max_exact = REL_BUCKETS // 2
    nf = jnp.maximum(n, 1).astype(F32)
    large = max_exact + (jnp.log(nf / max_exact) / math.log(REL_MAX_DIST / max_exact)
                         * (REL_BUCKETS - max_exact)).astype(I32)
    large = jnp.minimum(large, REL_BUCKETS - 1)
    return jnp.where(n < max_exact, n, large)


def _proj_kernel(x_ref, w_ref, *out_refs, pieces):
    xb = x_ref[...].astype(BF16)
    for o_ref, (lo, hi, scale) in zip(out_refs, pieces):
        acc = jnp.dot(xb, w_ref[:, lo:hi], preferred_element_type=F32)
        if scale != 1.0:
            acc = acc * scale
        o_ref[...] = acc.astype(BF16)


def _input_projection(x2d, w_bf16, tm=256):
    T, D = x2d.shape
    widths = (DIFF_QK_W, DIFF_QK_W, DIFF_V_W, DIL_W, DIL_W, DIL_W, D, D)
    scales = (DIFF_HD ** -0.5 * LOG2_E, 1.0, 1.0, DIL_HD ** -0.5 * LOG2_E, 1.0, 1.0, 1.0, 1.0)
    pieces, off = [], 0
    for w, s in zip(widths, scales):
        pieces.append((off, off + w, s))
        off += w
    assert off == w_bf16.shape[1]
    return pl.pallas_call(
        functools.partial(_proj_kernel, pieces=tuple(pieces)),
        grid=(T // tm,),
        in_specs=[pl.BlockSpec((tm, D), lambda i: (i, 0)),
                  pl.BlockSpec(w_bf16.shape, lambda i: (0, 0))],
        out_specs=[pl.BlockSpec((tm, w), lambda i: (i, 0)) for w in widths],
        out_shape=[jax.ShapeDtypeStruct((T, w), BF16) for w in widths],
        compiler_params=_cparams(("parallel",)),
        name="input_projection",
    )(x2d, w_bf16)


def _diff_kernel(lam_ref, q_ref, k_ref, v_ref, bias_ref, g_ref, o_ref, m_ref, l_ref, acc_ref, s_ref,
                 *, tq, tk, n_near, out_scale):
    i = pl.program_id(2)
    q = q_ref[...]
    lane = lax.broadcasted_iota(I32, q.shape, 1)
    zero = jnp.zeros_like(q)
    qq = jnp.concatenate([jnp.where(lane < DIFF_HD, q, zero), jnp.where(lane >= DIFF_HD, q, zero)], axis=0)

    m_ref[...] = jnp.full(m_ref.shape, MASK_VALUE, F32)
    l_ref[...] = jnp.zeros(l_ref.shape, F32)
    acc_ref[...] = jnp.zeros(acc_ref.shape, F32)

    def run(tiles):
        for n, (j, _) in enumerate(tiles):
            k = k_ref[pl.ds(pl.multiple_of(j * tk, tk), tk), :]
            for mi in range(2):
                s_ref[2 * n + mi] = lax.dot_general(k, qq[mi * tq:(mi + 1) * tq], (((1,), (1,)), ((), ())),
                                                    preferred_element_type=F32)
        for n, (j, bias) in enumerate(tiles):
            v = v_ref[pl.ds(pl.multiple_of(j * tk, tk), tk), :]
            for mi in range(2):
                cols = slice(mi * tq, (mi + 1) * tq)
                s = s_ref[2 * n + mi]
                if bias is not None:
                    s = s + bias
                m_prev = m_ref[:, cols]
                m_new = jnp.maximum(m_prev, jnp.max(s, axis=0, keepdims=True))
                alpha = jnp.exp2(m_prev - m_new)
                p = jnp.exp2(s - m_new)
                l_ref[:, cols] = alpha * l_ref[:, cols] + jnp.sum(p, axis=0, keepdims=True)
                pv = lax.dot_general(v, p.astype(BF16), (((0,), (0,)), ((), ())), preferred_element_type=F32)
                acc_ref[:, cols] = alpha * acc_ref[:, cols] + pv
                m_ref[:, cols] = m_new

    def near_tile(delta):
        return jnp.maximum(i - delta, 0), bias_ref[0, jnp.where(i >= delta, delta, n_near)]

    for a in range(0, n_near, 2):
        run([near_tile(delta) for delta in range(a, min(a + 2, n_near))])

    n_far = jnp.maximum(i - (n_near - 1), 0)

    @pl.when(n_far % 2 == 1)
    def _():
        run([(n_far - 1, None)])

    def far_body(jj, carry):
        run([(2 * jj, None), (2 * jj + 1, None)])
        return carry

    lax.fori_loop(0, n_far // 2, far_body, 0)

    o = acc_ref[...] / l_ref[...]
    out = o[:, :tq] - lam_ref[0] * o[:, tq:]
    out = out * lax.rsqrt(jnp.mean(out * out, axis=0, keepdims=True) + SUBLN_EPS)
    o_ref[...] = (out.T * g_ref[...] * out_scale).astype(o_ref.dtype)


def _toeplitz_tile(rel, offset, tk, tq):
    n_heads, n = rel.shape
    span = tk + tq - 1
    pad_r = max(offset + span - (tk - 1) - n, 0)
    relp = jnp.pad(rel, ((0, 0), (tk - 1, pad_r)), constant_values=MASK_VALUE)
    v = relp[:, offset:offset + span]
    v = jnp.pad(v, ((0, 0), (0, 1)))
    m = jnp.tile(v, (1, tk))[:, :tk * span].reshape(n_heads, tk, span)
    return m[:, :, tk - 1:tk - 1 + tq]


def _near_tile_count(t):
    n = np.arange(1, 4 * REL_MAX_DIST, dtype=np.float32)
    large = 16 + (np.log(n / 16) / math.log(REL_MAX_DIST / 16) * 16).astype(np.int32)
    first_const = int(np.max(np.nonzero(large < REL_BUCKETS - 1)[0])) + 2
    first_const += 16
    delta = 0
    while (delta - 1) * t + 1 < first_const:
        delta += 1
    return delta


def _diff_attention(dq, dk, dv, lam, lam_init, bias_dist, subln_g, B, S, tq=512):
    T = B * S
    tk = tq
    nq = S // tq
    n_near = min(_near_tile_count(tq), nq)
    far = bias_dist[S - 1] if n_near < nq else jnp.zeros_like(bias_dist[0])
    rel = ((bias_dist[:min(S, n_near * tq)] - far) * LOG2_E).T
    tiles = [_toeplitz_tile(rel, delta * tq, tk, tq) for delta in range(n_near)]
    tiles.append(jnp.full_like(tiles[0], MASK_VALUE))
    bias_tiles = jnp.stack(tiles, axis=1).astype(F32)

    kern = functools.partial(_diff_kernel, tq=tq, tk=tk, n_near=n_near, out_scale=1.0 - lam_init)
    return pl.pallas_call(
        kern,
        grid=(B, DIFF_HEADS, nq),
        in_specs=[pl.BlockSpec(memory_space=pltpu.SMEM),
                  pl.BlockSpec((tq, LANES), lambda b, h, i: (b * nq + i, h)),
                  pl.BlockSpec((S, LANES), lambda b, h, i: (b, h)),
                  pl.BlockSpec((S, LANES), lambda b, h, i: (b, h)),
                  pl.BlockSpec((1, n_near + 1, tk, tq), lambda b, h, i: (h, 0, 0, 0)),
                  pl.BlockSpec((1, DIFF_VD), lambda b, h, i: (0, 0))],
        out_specs=pl.BlockSpec((tq, LANES), lambda b, h, i: (b * nq + i, h)),
        out_shape=jax.ShapeDtypeStruct((T, DIFF_V_W), BF16),
        scratch_shapes=[pltpu.VMEM((1, 2 * tq), F32), pltpu.VMEM((1, 2 * tq), F32),
                        pltpu.VMEM((DIFF_VD, 2 * tq), F32), pltpu.VMEM((4, tk, tq), F32)],
        compiler_params=_cparams(("parallel", "parallel", "parallel")),
        name="diff_attention",
    )(lam.reshape(1), dq, dk, dv, bias_tiles, subln_g.reshape(1, DIFF_VD))


def _dil_blocks(t):
    return tuple((g, delta) for g, (window, _) in enumerate(DIL_PATTERNS) for delta in range(window // t + 1))


def _dil_kernel(*refs, t, blocks):
    nb = len(blocks)
    q_ref = refs[0]
    k_refs = refs[1:1 + nb]
    v_refs = refs[1 + nb:1 + 2 * nb]
    bias_ref = refs[1 + 2 * nb]
    o_ref = refs[2 + 2 * nb]
    kcat_ref, vcat_ref = refs[3 + 2 * nb:]
    i = pl.program_id(1)

    lane = lax.broadcasted_iota(I32, (t, DIL_GROUP_W), 1)
    head_masks = [(lane >= h * DIL_HD) & (lane < (h + 1) * DIL_HD) for h in range(DIL_HEADS)]

    scores = []
    for g in range(N_DIL_GROUPS):
        idx = [bi for bi, (bg, _) in enumerate(blocks) if bg == g]
        lo, n = idx[0] * t, len(idx) * t
        for bi in idx:
            kcat_ref[bi * t:(bi + 1) * t, :] = k_refs[bi][...]
            vcat_ref[bi * t:(bi + 1) * t, :] = v_refs[bi][...]
        q = q_ref[:, g * DIL_GROUP_W:(g + 1) * DIL_GROUP_W]
        zero = jnp.zeros_like(q)
        qs = jnp.concatenate([jnp.where(hm, q, zero) for hm in head_masks], axis=0)
        s = lax.dot_general(kcat_ref[lo:lo + n, :], qs, (((1,), (1,)), ((), ())),
                            preferred_element_type=F32) + bias_ref[lo:lo + n, :]
        delta_of_row = lax.broadcasted_iota(I32, s.shape, 0) // t
        scores.append(jnp.where(delta_of_row <= i, s, MASK_VALUE))

    m = functools.reduce(jnp.maximum, [jnp.max(s, axis=0, keepdims=True) for s in scores])
    l = jnp.zeros_like(m)
    acc = jnp.zeros((DIL_GROUP_W, DIL_HEADS * t), F32)
    for g, s in enumerate(scores):
        idx = [bi for bi, (bg, _) in enumerate(blocks) if bg == g]
        lo, n = idx[0] * t, len(idx) * t
        p = jnp.exp2(s - m)
        l = l + jnp.sum(p, axis=0, keepdims=True)
        acc = acc + lax.dot_general(vcat_ref[lo:lo + n, :], p.astype(BF16), (((0,), (0,)), ((), ())),
                                    preferred_element_type=F32)
    o = acc / l
    y_t = jnp.concatenate([o[h * DIL_HD:(h + 1) * DIL_HD, h * t:(h + 1) * t] for h in range(DIL_HEADS)], axis=0)
    o_ref[...] = y_t.T.astype(o_ref.dtype)


def _dilated_attention(lq, lk, lv, rel_bias_dil, B, S, t=128):
    T = B * S
    nq = S // t
    blocks = _dil_blocks(t)
    nb = len(blocks)
    max_w = max(w for w, _ in DIL_PATTERNS)
    bd = jnp.take(rel_bias_dil, _rel_bucket(jnp.arange(max_w + 1)), axis=0).astype(F32).T * LOG2_E
    dists = jnp.arange(max_w + 1)
    tiles = []
    for g, delta in blocks:
        window, dil = DIL_PATTERNS[g]
        rel = jnp.where((dists % dil == 0)[None, :window + 1],
                        bd[g * DIL_HEADS:(g + 1) * DIL_HEADS, :window + 1], MASK_VALUE)
        tile = _toeplitz_tile(rel, delta * t, t, t)
        tiles.append(jnp.moveaxis(tile, 0, 1).reshape(t, DIL_HEADS * t))
    bias_tiles = jnp.concatenate(tiles, axis=0)

    def kv_spec(g, delta):
        return pl.BlockSpec((t, DIL_GROUP_W), lambda b, i: (b * nq + jnp.maximum(i - delta, 0), g))

    kv_specs = [kv_spec(g, delta) for g, delta in blocks]
    return pl.pallas_call(
        functools.partial(_dil_kernel, t=t, blocks=blocks),
        grid=(B, nq),
        in_specs=[pl.BlockSpec((t, DIL_W), lambda b, i: (b * nq + i, 0))] + kv_specs + kv_specs
                 + [pl.BlockSpec(bias_tiles.shape, lambda b, i: (0, 0))],
        out_specs=pl.BlockSpec((t, DIL_GROUP_W), lambda b, i: (b * nq + i, 0)),
        out_shape=jax.ShapeDtypeStruct((T, DIL_GROUP_W), BF16),
        scratch_shapes=[pltpu.VMEM((nb * t, DIL_GROUP_W), BF16), pltpu.VMEM((nb * t, DIL_GROUP_W), BF16)],
        compiler_params=_cparams(("parallel", "parallel")),
        name="dilated_attention",
    )(lq, *([lk] * nb), *([lv] * nb), bias_tiles)


def _layer_norm(y, g, b):
    mu = jnp.mean(y, axis=-1, keepdims=True)
    var = jnp.mean(jnp.square(y - mu), axis=-1, keepdims=True)
    return (y - mu) * lax.rsqrt(var + LN_EPS) * g + b


def _post_kernel(x_ref, ya_ref, yb_ref, ga_ref, gb_ref, wd_ref, wl_ref, wo_ref, g1_ref, b1_ref, wq_ref,
                 x1_ref, pq_ref):
    a = jnp.dot(ya_ref[...], wd_ref[...], preferred_element_type=F32)
    b = jnp.dot(yb_ref[...], wl_ref[...], preferred_element_type=F32)
    mix = jax.nn.sigmoid(ga_ref[...].astype(F32)) * a + jax.nn.sigmoid(gb_ref[...].astype(F32)) * b
    mixed = jnp.dot(mix.astype(BF16), wo_ref[...], preferred_element_type=F32)
    x1 = _layer_norm(ALPHA * x_ref[...] + mixed, g1_ref[...], b1_ref[...])
    x1_ref[...] = x1
    pq_ref[...] = jnp.dot(x1.astype(BF16), wq_ref[...], preferred_element_type=F32).astype(BF16)


def _post_attention(x2d, ya, yb, ga, gb, wd, wl, wo, g1, b1, wq, tm=256):
    T, D = x2d.shape
    NQ = wq.shape[1]

    def row(w):
        return pl.BlockSpec((tm, w), lambda i: (i, 0))

    def full(a):
        return pl.BlockSpec(a.shape, lambda i: (0,) * a.ndim)

    return pl.pallas_call(
        _post_kernel,
        grid=(T // tm,),
        in_specs=[row(D), row(ya.shape[1]), row(yb.shape[1]), row(D), row(D),
                  full(wd), full(wl), full(wo), full(g1), full(b1), full(wq)],
        out_specs=[row(D), row(NQ)],
        out_shape=[jax.ShapeDtypeStruct((T, D), F32), jax.ShapeDtypeStruct((T, NQ), BF16)],
        compiler_params=_cparams(("parallel",)),
        name="post_attention",
    )(x2d, ya, yb, ga, gb, wd, wl, wo, g1, b1, wq)


def _top_rows(s, k, payload=None):
    n = s.shape[0]
    row = lax.broadcasted_iota(I32, s.shape, 0)
    vals, picks = [], []
    for _ in range(k):
        m = jnp.max(s, axis=0, keepdims=True)
        idx = jnp.min(jnp.where(s == m, row, n), axis=0, keepdims=True)
        hit = row == idx
        if payload is not None:
            idx = jnp.max(jnp.where(hit, payload, -1), axis=0, keepdims=True)
        s = jnp.where(hit, -jnp.inf, s)
        vals.append(m)
        picks.append(idx)
    return jnp.concatenate(vals, axis=0), jnp.concatenate(picks, axis=0)


def _candidate_ranks():
    return [(a, PEER_TOPK // (a + 1)) for a in range(PEER_TOPK)]


def _retrieve_kernel(pq_ref, keys_ref, row_ref, sh_ref, g_ref):
    e_rows, g_rows = [], []
    for h in range(PEER_HEADS):
        tops = []
        for c in range(2):
            col = (h * 2 + c) * PEER_DKEY
            qh = pq_ref[:, col:col + PEER_DKEY]
            s = lax.dot_general(keys_ref[h, c], qh, (((1,), (1,)), ((), ())),
                                preferred_element_type=F32)
            tops.append(_top_rows(s, PEER_TOPK))
        (s0, i0), (s1, i1) = tops
        ranks = _candidate_ranks()
        n_cand = sum(nb for _, nb in ranks)
        pad = -n_cand % SUBLANES
        cand = jnp.concatenate([s0[a:a + 1] + s1[:nb] for a, nb in ranks]
                               + [jnp.full((pad, s0.shape[1]), -jnp.inf, F32)], axis=0)
        cand_idx = jnp.concatenate([i0[a:a + 1] * PEER_NKEYS + i1[:nb] for a, nb in ranks]
                                   + [jnp.zeros((pad, s0.shape[1]), I32)], axis=0)
        best, eidx = _top_rows(cand, PEER_TOPK, payload=cand_idx)
        e = jnp.exp(best - jnp.max(best, axis=0, keepdims=True))
        g_rows.append(e / jnp.sum(e, axis=0, keepdims=True))
        e_rows.append(eidx)
    eidx = jnp.concatenate(e_rows, axis=0)
    row_ref[...] = ((eidx >> 1) * SUBLANES).T
    sh_ref[...] = (((eidx & 1) ^ 1) << 4).T
    g_ref[...] = jnp.concatenate(g_rows, axis=0).T


def _peer_retrieve(pq, keys_bf16, tt=128):
    T = pq.shape[0]
    out = pl.BlockSpec((tt, PEER_PAIRS), lambda i: (i, 0))
    return pl.pallas_call(
        _retrieve_kernel,
        grid=(T // tt,),
        in_specs=[pl.BlockSpec((tt, pq.shape[1]), lambda i: (i, 0)),
                  pl.BlockSpec(keys_bf16.shape, lambda i: (0, 0, 0, 0))],
        out_specs=[out, out, out],
        out_shape=[jax.ShapeDtypeStruct((T, PEER_PAIRS), I32), jax.ShapeDtypeStruct((T, PEER_PAIRS), I32),
                   jax.ShapeDtypeStruct((T, PEER_PAIRS), F32)],
        compiler_params=_cparams(("parallel",)),
        name="peer_retrieve",
    )(pq, keys_bf16)


def _pack_table(w):
    n, d = w.shape
    bits = lax.bitcast_convert_type(w.astype(BF16), jnp.uint16).astype(jnp.uint32)
    bits = bits.reshape(n // 2, 2, d)
    packed = bits[:, 0] | (bits[:, 1] << 16)
    return lax.bitcast_convert_type(packed, I32).reshape(n // 2 * (d // LANES), LANES)


def _expert_tile(tab_ref, row, shift):
    tile = tab_ref[pl.ds(pl.multiple_of(row, SUBLANES), SUBLANES), :]
    return pltpu.bitcast((tile << shift) & HI16, F32)


def _lane_rows(ref, t):
    return jnp.broadcast_to(ref[pl.ds(t, 1), :], (PEER_PAIRS, LANES)).T


def _two_token_pipeline(tt, prep, compute, finish):
    prep(0, 0)

    def body(i, carry):
        t0 = 2 * i
        prep(t0 + 1, 1)
        finish(jnp.maximum(t0 - 1, 0), 1)
        compute(t0, 0)
        prep(jnp.minimum(t0 + 2, tt - 1), 0)
        finish(t0, 0)
        compute(t0 + 1, 1)
        return carry

    lax.fori_loop(0, tt // 2, body, 0)
    finish(tt - 1, 1)


def _peer_u_kernel(row_ref, sh_ref, x_ref, tab_ref, h_ref, shb_ref, prod_ref, part_ref):
    tt = x_ref.shape[0]
    tile_rows = SUBLANES * SUBLANES

    def prep(t, slot):
        shb_ref[slot] = _lane_rows(sh_ref, t)

    def compute(t, slot):
        x = x_ref[t]
        for g in range(PEER_PAIRS // SUBLANES):
            base = g * tile_rows
            for j in range(SUBLANES):
                k = g * SUBLANES + j
                f = _expert_tile(tab_ref, row_ref[t, k], shb_ref[slot, k:k + 1, :])
                prod_ref[slot, base + j * SUBLANES:base + (j + 1) * SUBLANES, :] = f * x
            r = prod_ref[slot, pl.ds(base, SUBLANES, stride=SUBLANES), :]
            for c in range(1, SUBLANES):
                r = r + prod_ref[slot, pl.ds(base + c, SUBLANES, stride=SUBLANES), :]
            part_ref[slot, g * SUBLANES:(g + 1) * SUBLANES, :] = r

    def finish(t, slot):
        h_ref[pl.ds(t, 1), :] = jnp.sum(part_ref[slot].T, axis=0, keepdims=True)

    part_ref[1] = jnp.zeros(part_ref.shape[1:], F32)
    _two_token_pipeline(tt, prep, compute, finish)


def _peer_u(rows, shifts, x1_tiles, utab, tt=128):
    T = rows.shape[0]
    pair_spec = pl.BlockSpec((tt, PEER_PAIRS), lambda i: (i, 0))
    return pl.pallas_call(
        _peer_u_kernel,
        grid=(T // tt,),
        in_specs=[pl.BlockSpec((tt, PEER_PAIRS), lambda i: (i, 0), memory_space=pltpu.SMEM),
                  pair_spec,
                  pl.BlockSpec((tt, SUBLANES, LANES), lambda i: (i, 0, 0)),
                  pl.BlockSpec(memory_space=pltpu.VMEM)],
        out_specs=pair_spec,
        out_shape=jax.ShapeDtypeStruct((T, PEER_PAIRS), F32),
        scratch_shapes=[pltpu.VMEM((2, PEER_PAIRS, LANES), I32),
                        pltpu.VMEM((2, PEER_PAIRS * SUBLANES, LANES), F32),
                        pltpu.VMEM((2, PEER_PAIRS, LANES), F32)],
        compiler_params=_cparams(("arbitrary",)),
        name="peer_u",
    )(rows, shifts, x1_tiles, utab)


def _peer_v_kernel(row_ref, sh_ref, g_ref, h_ref, tab_ref, o_ref, w_ref, shb_ref, wb_ref):
    tt = o_ref.shape[0]
    n_acc = 4
    h = h_ref[...]
    w_ref[...] = g_ref[...] * (0.5 * h * (1.0 + lax.erf(h * (2.0 ** -0.5))))

    def prep(t, slot):
        shb_ref[slot] = _lane_rows(sh_ref, t)
        wb_ref[slot] = _lane_rows(w_ref, t)

    def compute(t, slot):
        accs = [None] * n_acc
        for k in range(PEER_PAIRS):
            f = _expert_tile(tab_ref, row_ref[t, k], shb_ref[slot, k:k + 1, :])
            term = wb_ref[slot, k:k + 1, :] * f
            accs[k % n_acc] = term if accs[k % n_acc] is None else accs[k % n_acc] + term
        o_ref[t] = (accs[0] + accs[1]) + (accs[2] + accs[3])

    def finish(t, slot):
        pass

    _two_token_pipeline(tt, prep, compute, finish)


def _peer_v(rows, shifts, gates, h, vtab, tt=128):
    T = rows.shape[0]
    pair_spec = pl.BlockSpec((tt, PEER_PAIRS), lambda i: (i, 0))
    return pl.pallas_call(
        _peer_v_kernel,
        grid=(T // tt,),
        in_specs=[pl.BlockSpec((tt, PEER_PAIRS), lambda i: (i, 0), memory_space=pltpu.SMEM),
                  pair_spec, pair_spec, pair_spec,
                  pl.BlockSpec(memory_space=pltpu.VMEM)],
        out_specs=pl.BlockSpec((tt, SUBLANES, LANES), lambda i: (i, 0, 0)),
        out_shape=jax.ShapeDtypeStruct((T, SUBLANES, LANES), F32),
        scratch_shapes=[pltpu.VMEM((tt, PEER_PAIRS), F32),
                        pltpu.VMEM((2, PEER_PAIRS, LANES), I32),
                        pltpu.VMEM((2, PEER_PAIRS, LANES), F32)],
        compiler_params=_cparams(("arbitrary",)),
        name="peer_v",
    )(rows, shifts, gates, h, vtab)


def _final_kernel(x1_ref, f_ref, g_ref, b_ref, o_ref):
    o_ref[...] = _layer_norm(ALPHA * x1_ref[...] + f_ref[...], g_ref[...], b_ref[...])


def _final_norm(x1, ffn, g, b, tm=512):
    T, D = x1.shape
    return pl.pallas_call(
        _final_kernel,
        grid=(T // tm,),
        in_specs=[pl.BlockSpec((tm, D), lambda i: (i, 0)), pl.BlockSpec((tm, D), lambda i: (i, 0)),
                  pl.BlockSpec((1, D), lambda i: (0, 0)), pl.BlockSpec((1, D), lambda i: (0, 0))],
        out_specs=pl.BlockSpec((tm, D), lambda i: (i, 0)),
        out_shape=jax.ShapeDtypeStruct((T, D), F32),
        compiler_params=_cparams(("parallel",)),
        name="final_norm",
    )(x1, ffn, g, b)


def kernel(x, w_in, diff_lambda, diff_subln_g, w_branch_diff, w_branch_dil, w_out, ln1_g, ln1_b,
           peer_wq, peer_keys, peer_u, peer_v, ln2_g, ln2_b, rel_bias):
    B, S, D = x.shape
    T = B * S
    assert w_in.shape[0] == DEPTH
    bias_dist = jnp.take(rel_bias, _rel_bucket(jnp.arange(S)), axis=0)
    x2d = x.reshape(T, D)
    for l in range(DEPTH):
        lam_init = 0.8 - 0.6 * math.exp(-0.3 * l)
        lp = diff_lambda[l].astype(F32)
        lam = jnp.exp(jnp.sum(lp[0] * lp[1])) - jnp.exp(jnp.sum(lp[2] * lp[3])) + lam_init

        dq, dk, dv, lq, lk, lv, ga, gb = _input_projection(x2d, w_in[l].astype(BF16))
        ya = _diff_attention(dq, dk, dv, lam, lam_init, bias_dist[:, :DIFF_HEADS], diff_subln_g[l], B, S)
        yb = _dilated_attention(lq, lk, lv, rel_bias[:, DIFF_HEADS:], B, S)
        x1, pq = _post_attention(
            x2d, ya, yb, ga, gb, w_branch_diff[l].astype(BF16), w_branch_dil[l].astype(BF16),
            w_out[l].astype(BF16), ln1_g[l].reshape(1, D), ln1_b[l].reshape(1, D), peer_wq[l].astype(BF16))

        rows, shifts, gates = _peer_retrieve(pq, peer_keys[l].astype(BF16))
        h = _peer_u(rows, shifts, x1.reshape(T, D // LANES, LANES), _pack_table(peer_u[l]))
        ffn = _peer_v(rows, shifts, gates, h, _pack_table(peer_v[l]))
        x2d = _final_norm(x1, ffn.reshape(T, D), ln2_g[l].reshape(1, D), ln2_b[l].reshape(1, D))
    return x2d.reshape(B, S, D)
```

```python
import functools
import math

import numpy as np
import jax
import jax.numpy as jnp
from jax import lax
from jax.experimental import pallas as pl
from jax.experimental.pallas import tpu as pltpu

F32 = jnp.float32
BF16 = jnp.bfloat16
I32 = jnp.int32

DEPTH = 1
DIFF_HEADS = 4
DIFF_HD = 64
DIFF_VD = 2 * DIFF_HD
DIL_PATTERNS = ((128, 1), (512, 4), (2048, 16))
N_DIL_GROUPS = len(DIL_PATTERNS)
DIL_HEADS = 4
DIL_HD = 64
REL_BUCKETS = 32
REL_MAX_DIST = 2048
PEER_HEADS = 8
PEER_NKEYS = 128
PEER_TOPK = 16
PEER_DKEY = 128
ALPHA = (2.0 * DEPTH) ** 0.25
LN_EPS = 1e-5
SUBLN_EPS = 1e-5

DIFF_QK_W = DIFF_HEADS * 2 * DIFF_HD
DIFF_V_W = DIFF_HEADS * DIFF_VD
DIL_W = N_DIL_GROUPS * DIL_HEADS * DIL_HD
DIL_GROUP_W = DIL_HEADS * DIL_HD

LANES = 128
SUBLANES = 8
HALF_SUBLANES = SUBLANES // 2
VMEM_LIMIT = 56 * 1024 * 1024

MASK_VALUE = -1e30

PEER_PAIRS = PEER_HEADS * PEER_TOPK
STAGE_TOKENS = 8
HI16 = -65536
LOG2_E = math.log2(math.e)
FAR_TILES_PER_STEP = 4


def _cparams(sem):
    return pltpu.CompilerParams(dimension_semantics=sem, vmem_limit_bytes=VMEM_LIMIT)


def _rel_bucket(dist):
    n = jnp.maximum(dist, 0)
    max_exact = REL_BUCKETS // 2
    nf = jnp.maximum(n, 1).astype(F32)
    large = max_exact + (jnp.log(nf / max_exact) / math.log(REL_MAX_DIST / max_exact)
                         * (REL_BUCKETS - max_exact)).astype(I32)
    large = jnp.minimum(large, REL_BUCKETS - 1)
    return jnp.where(n < max_exact, n, large)


def _proj_kernel(x_ref, w_ref, *out_refs, pieces):
    xb = x_ref[...].astype(BF16)
    for o_ref, (lo, hi, scale) in zip(out_refs, pieces):
        acc = jnp.dot(xb, w_ref[:, lo:hi], preferred_element_type=F32)
        if scale != 1.0:
            acc = acc * scale
        o_ref[...] = acc.astype(BF16)


def _input_projection(x2d, w_bf16, tm=256):
    T, D = x2d.shape
    widths = (DIFF_QK_W, DIFF_QK_W, DIFF_V_W, DIL_W, DIL_W, DIL_W, D, D)
    scales = (DIFF_HD ** -0.5 * LOG2_E, 1.0, 1.0, DIL_HD ** -0.5 * LOG2_E, 1.0, 1.0, 1.0, 1.0)
    pieces, off = [], 0
    for w, s in zip(widths, scales):
        pieces.append((off, off + w, s))
        off += w
    assert off == w_bf16.shape[1]
    return pl.pallas_call(
        functools.partial(_proj_kernel, pieces=tuple(pieces)),
        grid=(T // tm,),
        in_specs=[pl.BlockSpec((tm, D), lambda i: (i, 0)),
                  pl.BlockSpec(w_bf16.shape, lambda i: (0, 0))],
        out_specs=[pl.BlockSpec((tm, w), lambda i: (i, 0)) for w in widths],
        out_shape=[jax.ShapeDtypeStruct((T, w), BF16) for w in widths],
        compiler_params=_cparams(("parallel",)),
        name="input_projection",
    )(x2d, w_bf16)


def _diff_kernel(lam_ref, q_ref, k_ref, v_ref, bias_ref, g_ref, o_ref, m_ref, l_ref, acc_ref, s_ref,
                 *, tq, tk, n_near, out_scale):
    i = pl.program_id(2)
    q = q_ref[...]
    lane = lax.broadcasted_iota(I32, q.shape, 1)
    zero = jnp.zeros_like(q)
    qq = jnp.concatenate([jnp.where(lane < DIFF_HD, q, zero), jnp.where(lane >= DIFF_HD, q, zero)], axis=0)

    m_ref[...] = jnp.full(m_ref.shape, MASK_VALUE, F32)
    l_ref[...] = jnp.zeros(l_ref.shape, F32)
    acc_ref[...] = jnp.zeros(acc_ref.shape, F32)

    def run(tiles):
        for n, (j, _) in enumerate(tiles):
            k = k_ref[pl.ds(pl.multiple_of(j * tk, tk), tk), :]
            for mi in range(2):
                s_ref[2 * n + mi] = lax.dot_general(k, qq[mi * tq:(mi + 1) * tq], (((1,), (1,)), ((), ())),
                                                    preferred_element_type=F32)
        for n, (j, bias) in enumerate(tiles):
            v = v_ref[pl.ds(pl.multiple_of(j * tk, tk), tk), :]
            for mi in range(2):
                cols = slice(mi * tq, (mi + 1) * tq)
                s = s_ref[2 * n + mi]
                if bias is not None:
                    s = s + bias
                m_prev = m_ref[:, cols]
                m_new = jnp.maximum(m_prev, jnp.max(s, axis=0, keepdims=True))
                alpha = jnp.exp2(m_prev - m_new)
                p = jnp.exp2(s - m_new)
                l_ref[:, cols] = alpha * l_ref[:, cols] + jnp.sum(p, axis=0, keepdims=True)
                pv = lax.dot_general(v, p.astype(BF16), (((0,), (0,)), ((), ())), preferred_element_type=F32)
                acc_ref[:, cols] = alpha * acc_ref[:, cols] + pv
                m_ref[:, cols] = m_new

    def near_tile(delta):
        return jnp.maximum(i - delta, 0), bias_ref[0, jnp.where(i >= delta, delta, n_near)]

    for a in range(0, n_near, 2):
        run([near_tile(delta) for delta in range(a, min(a + 2, n_near))])

    n_far = jnp.maximum(i - (n_near - 1), 0)

    n_group = FAR_TILES_PER_STEP
    n_loop = n_far // n_group

    def far_body(jj, carry):
        run([(n_group * jj + n, None) for n in range(n_group)])
        return carry

    lax.fori_loop(0, n_loop, far_body, 0)

    done = n_loop * n_group
    size = n_group // 2
    while size >= 1:
        take = ((n_far - done) & size) != 0

        @pl.when(take)
        def _(done=done, size=size):
            run([(done + n, None) for n in range(size)])

        done = done + jnp.where(take, size, 0)
        size //= 2

    o = acc_ref[...] / l_ref[...]
    out = o[:, :tq] - lam_ref[0] * o[:, tq:]
    out = out * lax.rsqrt(jnp.mean(out * out, axis=0, keepdims=True) + SUBLN_EPS)
    o_ref[...] = (out.T * g_ref[...] * out_scale).astype(o_ref.dtype)


def _toeplitz_tile(rel, offset, tk, tq):
    n_heads, n = rel.shape
    span = tk + tq - 1
    pad_r = max(offset + span - (tk - 1) - n, 0)
    relp = jnp.pad(rel, ((0, 0), (tk - 1, pad_r)), constant_values=MASK_VALUE)
    v = relp[:, offset:offset + span]
    v = jnp.pad(v, ((0, 0), (0, 1)))
    m = jnp.tile(v, (1, tk))[:, :tk * span].reshape(n_heads, tk, span)
    return m[:, :, tk - 1:tk - 1 + tq]


def _near_tile_count(t):
    n = np.arange(1, 4 * REL_MAX_DIST, dtype=np.float32)
    large = 16 + (np.log(n / 16) / math.log(REL_MAX_DIST / 16) * 16).astype(np.int32)
    first_const = int(np.max(np.nonzero(large < REL_BUCKETS - 1)[0])) + 2
    first_const += 16
    delta = 0
    while (delta - 1) * t + 1 < first_const:
        delta += 1
    return delta


def _diff_attention(dq, dk, dv, lam, lam_init, bias_dist, subln_g, B, S, tq=512):
    T = B * S
    tk = tq
    nq = S // tq
    n_near = min(_near_tile_count(tq), nq)
    far = bias_dist[S - 1] if n_near < nq else jnp.zeros_like(bias_dist[0])
    rel = ((bias_dist[:min(S, n_near * tq)] - far) * LOG2_E).T
    tiles = [_toeplitz_tile(rel, delta * tq, tk, tq) for delta in range(n_near)]
    tiles.append(jnp.full_like(tiles[0], MASK_VALUE))
    bias_tiles = jnp.stack(tiles, axis=1).astype(F32)

    kern = functools.partial(_diff_kernel, tq=tq, tk=tk, n_near=n_near, out_scale=1.0 - lam_init)
    return pl.pallas_call(
        kern,
        grid=(B, DIFF_HEADS, nq),
        in_specs=[pl.BlockSpec(memory_space=pltpu.SMEM),
                  pl.BlockSpec((tq, LANES), lambda b, h, i: (b * nq + i, h)),
                  pl.BlockSpec((S, LANES), lambda b, h, i: (b, h)),
                  pl.BlockSpec((S, LANES), lambda b, h, i: (b, h)),
                  pl.BlockSpec((1, n_near + 1, tk, tq), lambda b, h, i: (h, 0, 0, 0)),
                  pl.BlockSpec((1, DIFF_VD), lambda b, h, i: (0, 0))],
        out_specs=pl.BlockSpec((tq, LANES), lambda b, h, i: (b * nq + i, h)),
        out_shape=jax.ShapeDtypeStruct((T, DIFF_V_W), BF16),
        scratch_shapes=[pltpu.VMEM((1, 2 * tq), F32), pltpu.VMEM((1, 2 * tq), F32),
                        pltpu.VMEM((DIFF_VD, 2 * tq), F32), pltpu.VMEM((2 * FAR_TILES_PER_STEP, tk, tq), F32)],
        compiler_params=_cparams(("parallel", "parallel", "parallel")),
        name="diff_attention",
    )(lam.reshape(1), dq, dk, dv, bias_tiles, subln_g.reshape(1, DIFF_VD))


def _dil_blocks(t):
    return tuple((g, delta) for g, (window, _) in enumerate(DIL_PATTERNS) for delta in range(window // t + 1))


def _dil_kernel(*refs, t, blocks):
    nb = len(blocks)
    q_ref = refs[0]
    k_refs = refs[1:1 + nb]
    v_refs = refs[1 + nb:1 + 2 * nb]
    bias_ref = refs[1 + 2 * nb]
    o_ref = refs[2 + 2 * nb]
    kcat_ref, vcat_ref = refs[3 + 2 * nb:]
    i = pl.program_id(1)

    lane = lax.broadcasted_iota(I32, (t, DIL_GROUP_W), 1)
    head_masks = [(lane >= h * DIL_HD) & (lane < (h + 1) * DIL_HD) for h in range(DIL_HEADS)]

    scores = []
    for g in range(N_DIL_GROUPS):
        idx = [bi for bi, (bg, _) in enumerate(blocks) if bg == g]
        lo, n = idx[0] * t, len(idx) * t
        for bi in idx:
            kcat_ref[bi * t:(bi + 1) * t, :] = k_refs[bi][...]
            vcat_ref[bi * t:(bi + 1) * t, :] = v_refs[bi][...]
        q = q_ref[:, g * DIL_GROUP_W:(g + 1) * DIL_GROUP_W]
        zero = jnp.zeros_like(q)
        qs = jnp.concatenate([jnp.where(hm, q, zero) for hm in head_masks], axis=0)
        s = lax.dot_general(kcat_ref[lo:lo + n, :], qs, (((1,), (1,)), ((), ())),
                            preferred_element_type=F32) + bias_ref[lo:lo + n, :]
        delta_of_row = lax.broadcasted_iota(I32, s.shape, 0) // t
        scores.append(jnp.where(delta_of_row <= i, s, MASK_VALUE))

    m = functools.reduce(jnp.maximum, [jnp.max(s, axis=0, keepdims=True) for s in scores])
    l = jnp.zeros_like(m)
    acc = jnp.zeros((DIL_GROUP_W, DIL_HEADS * t), F32)
    for g, s in enumerate(scores):
        idx = [bi for bi, (bg, _) in enumerate(blocks) if bg == g]
        lo, n = idx[0] * t, len(idx) * t
        p = jnp.exp2(s - m)
        l = l + jnp.sum(p, axis=0, keepdims=True)
        acc = acc + lax.dot_general(vcat_ref[lo:lo + n, :], p.astype(BF16), (((0,), (0,)), ((), ())),
                                    preferred_element_type=F32)
    o = acc / l
    y_t = jnp.concatenate([o[h * DIL_HD:(h + 1) * DIL_HD, h * t:(h + 1) * t] for h in range(DIL_HEADS)], axis=0)
    o_ref[...] = y_t.T.astype(o_ref.dtype)


def _dilated_attention(lq, lk, lv, rel_bias_dil, B, S, t=128):
    T = B * S
    nq = S // t
    blocks = _dil_blocks(t)
    nb = len(blocks)
    max_w = max(w for w, _ in DIL_PATTERNS)
    bd = jnp.take(rel_bias_dil, _rel_bucket(jnp.arange(max_w + 1)), axis=0).astype(F32).T * LOG2_E
    dists = jnp.arange(max_w + 1)
    tiles = []
    for g, delta in blocks:
        window, dil = DIL_PATTERNS[g]
        rel = jnp.where((dists % dil == 0)[None, :window + 1],
                        bd[g * DIL_HEADS:(g + 1) * DIL_HEADS, :window + 1], MASK_VALUE)
        tile = _toeplitz_tile(rel, delta * t, t, t)
        tiles.append(jnp.moveaxis(tile, 0, 1).reshape(t, DIL_HEADS * t))
    bias_tiles = jnp.concatenate(tiles, axis=0)

    def kv_spec(g, delta):
        return pl.BlockSpec((t, DIL_GROUP_W), lambda b, i: (b * nq + jnp.maximum(i - delta, 0), g))

    kv_specs = [kv_spec(g, delta) for g, delta in blocks]
    return pl.pallas_call(
        functools.partial(_dil_kernel, t=t, blocks=blocks),
        grid=(B, nq),
        in_specs=[pl.BlockSpec((t, DIL_W), lambda b, i: (b * nq + i, 0))] + kv_specs + kv_specs
                 + [pl.BlockSpec(bias_tiles.shape, lambda b, i: (0, 0))],
        out_specs=pl.BlockSpec((t, DIL_GROUP_W), lambda b, i: (b * nq + i, 0)),
        out_shape=jax.ShapeDtypeStruct((T, DIL_GROUP_W), BF16),
        scratch_shapes=[pltpu.VMEM((nb * t, DIL_GROUP_W), BF16), pltpu.VMEM((nb * t, DIL_GROUP_W), BF16)],
        compiler_params=_cparams(("parallel", "parallel")),
        name="dilated_attention",
    )(lq, *([lk] * nb), *([lv] * nb), bias_tiles)


def _layer_norm(y, g, b):
    mu = jnp.mean(y, axis=-1, keepdims=True)
    var = jnp.mean(jnp.square(y - mu), axis=-1, keepdims=True)
    return (y - mu) * lax.rsqrt(var + LN_EPS) * g + b


def _post_kernel(x_ref, ya_ref, yb_ref, ga_ref, gb_ref, wd_ref, wl_ref, wo_ref, g1_ref, b1_ref, wq_ref,
                 x1_ref, pq_ref):
    a = jnp.dot(ya_ref[...], wd_ref[...], preferred_element_type=F32)
    b = jnp.dot(yb_ref[...], wl_ref[...], preferred_element_type=F32)
    mix = jax.nn.sigmoid(ga_ref[...].astype(F32)) * a + jax.nn.sigmoid(gb_ref[...].astype(F32)) * b
    mixed = jnp.dot(mix.astype(BF16), wo_ref[...], preferred_element_type=F32)
    x1 = _layer_norm(ALPHA * x_ref[...] + mixed, g1_ref[...], b1_ref[...])
    x1_ref[...] = x1
    pq_ref[...] = jnp.dot(x1.astype(BF16), wq_ref[...], preferred_element_type=F32).astype(BF16)


def _post_attention(x2d, ya, yb, ga, gb, wd, wl, wo, g1, b1, wq, tm=256):
    T, D = x2d.shape
    NQ = wq.shape[1]

    def row(w):
        return pl.BlockSpec((tm, w), lambda i: (i, 0))

    def full(a):
        return pl.BlockSpec(a.shape, lambda i: (0,) * a.ndim)

    return pl.pallas_call(
        _post_kernel,
        grid=(T // tm,),
        in_specs=[row(D), row(ya.shape[1]), row(yb.shape[1]), row(D), row(D),
                  full(wd), full(wl), full(wo), full(g1), full(b1), full(wq)],
        out_specs=[row(D), row(NQ)],
        out_shape=[jax.ShapeDtypeStruct((T, D), F32), jax.ShapeDtypeStruct((T, NQ), BF16)],
        compiler_params=_cparams(("parallel",)),
        name="post_attention",
    )(x2d, ya, yb, ga, gb, wd, wl, wo, g1, b1, wq)


def _top_rows(s, k, payload=None):
    n = s.shape[0]
    row = lax.broadcasted_iota(I32, s.shape, 0)
    vals, picks = [], []
    for _ in range(k):
        m = jnp.max(s, axis=0, keepdims=True)
        idx = jnp.min(jnp.where(s == m, row, n), axis=0, keepdims=True)
        hit = row == idx
        if payload is not None:
            idx = jnp.max(jnp.where(hit, payload, -1), axis=0, keepdims=True)
        s = jnp.where(hit, -jnp.inf, s)
        vals.append(m)
        picks.append(idx)
    return jnp.concatenate(vals, axis=0), jnp.concatenate(picks, axis=0)


def _candidate_ranks():
    return [(a, PEER_TOPK // (a + 1)) for a in range(PEER_TOPK)]


def _retrieve_kernel(pq_ref, keys_ref, row_ref, sh_ref, g_ref):
    e_rows, g_rows = [], []
    for h in range(PEER_HEADS):
        tops = []
        for c in range(2):
            col = (h * 2 + c) * PEER_DKEY
            qh = pq_ref[:, col:col + PEER_DKEY]
            s = lax.dot_general(keys_ref[h, c], qh, (((1,), (1,)), ((), ())),
                                preferred_element_type=F32)
            tops.append(_top_rows(s, PEER_TOPK))
        (s0, i0), (s1, i1) = tops
        ranks = _candidate_ranks()
        n_cand = sum(nb for _, nb in ranks)
        pad = -n_cand % SUBLANES
        cand = jnp.concatenate([s0[a:a + 1] + s1[:nb] for a, nb in ranks]
                               + [jnp.full((pad, s0.shape[1]), -jnp.inf, F32)], axis=0)
        cand_idx = jnp.concatenate([i0[a:a + 1] * PEER_NKEYS + i1[:nb] for a, nb in ranks]
                                   + [jnp.zeros((pad, s0.shape[1]), I32)], axis=0)
        best, eidx = _top_rows(cand, PEER_TOPK, payload=cand_idx)
        e = jnp.exp(best - jnp.max(best, axis=0, keepdims=True))
        g_rows.append(e / jnp.sum(e, axis=0, keepdims=True))
        e_rows.append(eidx)
    eidx = jnp.concatenate(e_rows, axis=0)
    row_ref[...] = ((eidx >> 1) * SUBLANES).T
    sh_ref[...] = (((eidx & 1) ^ 1) << 4).T
    g_ref[...] = jnp.concatenate(g_rows, axis=0).T


def _peer_retrieve(pq, keys_bf16, tt=128):
    T = pq.shape[0]
    out = pl.BlockSpec((tt, PEER_PAIRS), lambda i: (i, 0))
    return pl.pallas_call(
        _retrieve_kernel,
        grid=(T // tt,),
        in_specs=[pl.BlockSpec((tt, pq.shape[1]), lambda i: (i, 0)),
                  pl.BlockSpec(keys_bf16.shape, lambda i: (0, 0, 0, 0))],
        out_specs=[out, out, out],
        out_shape=[jax.ShapeDtypeStruct((T, PEER_PAIRS), I32), jax.ShapeDtypeStruct((T, PEER_PAIRS), I32),
                   jax.ShapeDtypeStruct((T, PEER_PAIRS), F32)],
        compiler_params=_cparams(("parallel",)),
        name="peer_retrieve",
    )(pq, keys_bf16)


def _pack_table(w):
    n, d = w.shape
    bits = lax.bitcast_convert_type(w.astype(BF16), jnp.uint16).astype(jnp.uint32)
    bits = bits.reshape(n // 2, 2, d)
    packed = bits[:, 0] | (bits[:, 1] << 16)
    return lax.bitcast_convert_type(packed, I32).reshape(n // 2 * (d // LANES), LANES)


def _expert_tile(tab_ref, row, shift):
    tile = tab_ref[pl.ds(pl.multiple_of(row, SUBLANES), SUBLANES), :]
    return pltpu.bitcast((tile << shift) & HI16, F32)


def _lane_rows(ref, t):
    return jnp.broadcast_to(ref[pl.ds(t, 1), :], (PEER_PAIRS, LANES)).T


def _staged_token_loop(rows_ref, idx_ref, sems, tt, prep, compute, finish):
    n_chunks = tt // STAGE_TOKENS

    def stage(chunk, slot):
        src = rows_ref.at[pl.ds(pl.multiple_of(chunk * STAGE_TOKENS, STAGE_TOKENS), STAGE_TOKENS), :]
        return pltpu.make_async_copy(src, idx_ref.at[slot], sems.at[slot])

    stage(0, 0).start()
    stage(1, 1).start()
    prep(0, 0)

    def body(i, carry):
        for slot in range(2):
            chunk = 2 * i + slot
            stage(chunk, slot).wait()
            for g in range(STAGE_TOKENS):
                t = chunk * STAGE_TOKENS + g
                vslot = g % 2
                prep(jnp.minimum(t + 1, tt - 1), 1 - vslot)
                finish(jnp.maximum(t - 1, 0), 1 - vslot)
                compute(t, vslot, functools.partial(lambda s, gg, k: idx_ref[s, gg, k], slot, g))
            stage(jnp.minimum(chunk + 2, n_chunks - 1), slot).start()
        return carry

    lax.fori_loop(0, n_chunks // 2, body, 0)
    stage(n_chunks - 1, 0).wait()
    stage(n_chunks - 1, 1).wait()
    finish(tt - 1, (tt - 1) % 2)


def _peer_u_kernel(row_ref, sh_ref, x_ref, tab_ref, h_ref, shb_ref, prod_ref, part_ref, idx_ref, sems):
    tt = x_ref.shape[0]
    tile_rows = SUBLANES * SUBLANES

    def prep(t, slot):
        shb_ref[slot] = _lane_rows(sh_ref, t)

    def compute(t, slot, row_of):
        x = x_ref[t]
        for g in range(PEER_PAIRS // SUBLANES):
            base = g * tile_rows
            for j in range(SUBLANES):
                k = g * SUBLANES + j
                f = _expert_tile(tab_ref, row_of(k), shb_ref[slot, k:k + 1, :])
                pos = j // 2 + (SUBLANES // 2) * (j % 2)
                prod_ref[slot, base + pos * SUBLANES:base + (pos + 1) * SUBLANES, :] = f * x
            quads = []
            for half in range(2):
                hb = base + half * (tile_rows // 2)
                r4 = prod_ref[slot, pl.ds(hb, SUBLANES, stride=HALF_SUBLANES), :]
                for c in range(1, HALF_SUBLANES):
                    r4 = r4 + prod_ref[slot, pl.ds(hb + c, SUBLANES, stride=HALF_SUBLANES), :]
                quads.append(r4 + pltpu.roll(r4, SUBLANES - 1, axis=0))
            even = lax.broadcasted_iota(I32, (SUBLANES, LANES), 0) % 2 == 0
            part_ref[slot, g * SUBLANES:(g + 1) * SUBLANES, :] = jnp.where(
                even, quads[0], pltpu.roll(quads[1], 1, axis=0))

    def finish(t, slot):
        h_ref[pl.ds(t, 1), :] = jnp.sum(part_ref[slot].T, axis=0, keepdims=True)

    part_ref[1] = jnp.zeros(part_ref.shape[1:], F32)
    _staged_token_loop(row_ref, idx_ref, sems, tt, prep, compute, finish)


def _stage_scratch():
    return [pltpu.SMEM((2, STAGE_TOKENS, PEER_PAIRS), I32), pltpu.SemaphoreType.DMA((2,))]


def _peer_u(rows, shifts, x1_tiles, utab, tt=128):
    T = rows.shape[0]
    pair_spec = pl.BlockSpec((tt, PEER_PAIRS), lambda i: (i, 0))
    return pl.pallas_call(
        _peer_u_kernel,
        grid=(T // tt,),
        in_specs=[pair_spec, pair_spec,
                  pl.BlockSpec((tt, SUBLANES, LANES), lambda i: (i, 0, 0)),
                  pl.BlockSpec(memory_space=pltpu.VMEM)],
        out_specs=pair_spec,
        out_shape=jax.ShapeDtypeStruct((T, PEER_PAIRS), F32),
        scratch_shapes=[pltpu.VMEM((2, PEER_PAIRS, LANES), I32),
                        pltpu.VMEM((2, PEER_PAIRS * SUBLANES, LANES), F32),
                        pltpu.VMEM((2, PEER_PAIRS, LANES), F32)] + _stage_scratch(),
        compiler_params=_cparams(("arbitrary",)),
        name="peer_u",
    )(rows, shifts, x1_tiles, utab)


def _peer_v_kernel(row_ref, sh_ref, g_ref, h_ref, tab_ref, o_ref, w_ref, shb_ref, wb_ref, idx_ref, sems):
    tt = o_ref.shape[0]
    n_acc = 4
    h = h_ref[...]
    w_ref[...] = g_ref[...] * (0.5 * h * (1.0 + lax.erf(h * (2.0 ** -0.5))))

    def prep(t, slot):
        shb_ref[slot] = _lane_rows(sh_ref, t)
        wb_ref[slot] = _lane_rows(w_ref, t)

    def compute(t, slot, row_of):
        accs = [None] * n_acc
        for k in range(PEER_PAIRS):
            f = _expert_tile(tab_ref, row_of(k), shb_ref[slot, k:k + 1, :])
            term = wb_ref[slot, k:k + 1, :] * f
            accs[k % n_acc] = term if accs[k % n_acc] is None else accs[k % n_acc] + term
        o_ref[t] = (accs[0] + accs[1]) + (accs[2] + accs[3])

    def finish(t, slot):
        pass

    _staged_token_loop(row_ref, idx_ref, sems, tt, prep, compute, finish)


def _peer_v(rows, shifts, gates, h, vtab, tt=128):
    T = rows.shape[0]
    pair_spec = pl.BlockSpec((tt, PEER_PAIRS), lambda i: (i, 0))
    return pl.pallas_call(
        _peer_v_kernel,
        grid=(T // tt,),
        in_specs=[pair_spec, pair_spec, pair_spec, pair_spec,
                  pl.BlockSpec(memory_space=pltpu.VMEM)],
        out_specs=pl.BlockSpec((tt, SUBLANES, LANES), lambda i: (i, 0, 0)),
        out_shape=jax.ShapeDtypeStruct((T, SUBLANES, LANES), F32),
        scratch_shapes=[pltpu.VMEM((tt, PEER_PAIRS), F32),
                        pltpu.VMEM((2, PEER_PAIRS, LANES), I32),
                        pltpu.VMEM((2, PEER_PAIRS, LANES), F32)] + _stage_scratch(),
        compiler_params=_cparams(("arbitrary",)),
        name="peer_v",
    )(rows, shifts, gates, h, vtab)


def _final_kernel(x1_ref, f_ref, g_ref, b_ref, o_ref):
    o_ref[...] = _layer_norm(ALPHA * x1_ref[...] + f_ref[...], g_ref[...], b_ref[...])


def _final_norm(x1, ffn, g, b, tm=512):
    T, D = x1.shape
    return pl.pallas_call(
        _final_kernel,
        grid=(T // tm,),
        in_specs=[pl.BlockSpec((tm, D), lambda i: (i, 0)), pl.BlockSpec((tm, D), lambda i: (i, 0)),
                  pl.BlockSpec((1, D), lambda i: (0, 0)), pl.BlockSpec((1, D), lambda i: (0, 0))],
        out_specs=pl.BlockSpec((tm, D), lambda i: (i, 0)),
        out_shape=jax.ShapeDtypeStruct((T, D), F32),
        compiler_params=_cparams(("parallel",)),
        name="final_norm",
    )(x1, ffn, g, b)


def kernel(x, w_in, diff_lambda, diff_subln_g, w_branch_diff, w_branch_dil, w_out, ln1_g, ln1_b,
           peer_wq, peer_keys, peer_u, peer_v, ln2_g, ln2_b, rel_bias):
    B, S, D = x.shape
    T = B * S
    assert w_in.shape[0] == DEPTH
    bias_dist = jnp.take(rel_bias, _rel_bucket(jnp.arange(S)), axis=0)
    x2d = x.reshape(T, D)
    for l in range(DEPTH):
        lam_init = 0.8 - 0.6 * math.exp(-0.3 * l)
        lp = diff_lambda[l].astype(F32)
        lam = jnp.exp(jnp.sum(lp[0] * lp[1])) - jnp.exp(jnp.sum(lp[2] * lp[3])) + lam_init

        dq, dk, dv, lq, lk, lv, ga, gb = _input_projection(x2d, w_in[l].astype(BF16))
        ya = _diff_attention(dq, dk, dv, lam, lam_init, bias_dist[:, :DIFF_HEADS], diff_subln_g[l], B, S)
        yb = _dilated_attention(lq, lk, lv, rel_bias[:, DIFF_HEADS:], B, S)
        x1, pq = _post_attention(
            x2d, ya, yb, ga, gb, w_branch_diff[l].astype(BF16), w_branch_dil[l].astype(BF16),
            w_out[l].astype(BF16), ln1_g[l].reshape(1, D), ln1_b[l].reshape(1, D), peer_wq[l].astype(BF16))

        rows, shifts, gates = _peer_retrieve(pq, peer_keys[l].astype(BF16))
        h = _peer_u(rows, shifts, x1.reshape(T, D // LANES, LANES), _pack_table(peer_u[l]))
        ffn = _peer_v(rows, shifts, gates, h, _pack_table(peer_v[l]))
        x2d = _final_norm(x1, ffn.reshape(T, D), ln2_g[l].reshape(1, D), ln2_b[l].reshape(1, D))
    return x2d.reshape(B, S, D)
```

```python
import functools
import math

import numpy as np
import jax
import jax.numpy as jnp
from jax import lax
from jax.experimental import pallas as pl
from jax.experimental.pallas import tpu as pltpu

F32 = jnp.float32
BF16 = jnp.bfloat16
I32 = jnp.int32

DEPTH = 1
DIFF_HEADS = 4
DIFF_HD = 64
DIFF_VD = 2 * DIFF_HD
DIL_PATTERNS = ((128, 1), (512, 4), (2048, 16))
N_DIL_GROUPS = len(DIL_PATTERNS)
DIL_HEADS = 4
DIL_HD = 64
REL_BUCKETS = 32
REL_MAX_DIST = 2048
PEER_HEADS = 8
PEER_NKEYS = 128
PEER_TOPK = 16
PEER_DKEY = 128
ALPHA = (2.0 * DEPTH) ** 0.25
LN_EPS = 1e-5
SUBLN_EPS = 1e-5

DIFF_QK_W = DIFF_HEADS * 2 * DIFF_HD
DIFF_V_W = DIFF_HEADS * DIFF_VD
DIL_W = N_DIL_GROUPS * DIL_HEADS * DIL_HD
DIL_GROUP_W = DIL_HEADS * DIL_HD

LANES = 128
SUBLANES = 8
HALF_SUBLANES = SUBLANES // 2
VMEM_LIMIT = 56 * 1024 * 1024

MASK_VALUE = -1e30

PEER_PAIRS = PEER_HEADS * PEER_TOPK
STAGE_TOKENS = 16
HI16 = -65536
LOG2_E = math.log2(math.e)
FAR_TILES_PER_STEP = 8


def _cparams(sem):
    return pltpu.CompilerParams(dimension_semantics=sem, vmem_limit_bytes=VMEM_LIMIT)


def _rel_bucket(dist):
    n = jnp.maximum(dist, 0)
    max_exact = REL_BUCKETS // 2
    nf = jnp.maximum(n, 1).astype(F32)
    large = max_exact + (jnp.log(nf / max_exact) / math.log(REL_MAX_DIST / max_exact)
                         * (REL_BUCKETS - max_exact)).astype(I32)
    large = jnp.minimum(large, REL_BUCKETS - 1)
    return jnp.where(n < max_exact, n, large)


def _proj_kernel(x_ref, w_ref, *out_refs, pieces):
    xb = x_ref[...].astype(BF16)
    for o_ref, (lo, hi, scale) in zip(out_refs, pieces):
        acc = jnp.dot(xb, w_ref[:, lo:hi], preferred_element_type=F32)
        if scale != 1.0:
            acc = acc * scale
        o_ref[...] = acc.astype(BF16)


def _input_projection(x2d, w_bf16, tm=256):
    T, D = x2d.shape
    widths = (DIFF_QK_W, DIFF_QK_W, DIFF_V_W, DIL_W, DIL_W, DIL_W, D, D)
    scales = (DIFF_HD ** -0.5 * LOG2_E, 1.0, 1.0, DIL_HD ** -0.5 * LOG2_E, 1.0, 1.0, 1.0, 1.0)
    pieces, off = [], 0
    for w, s in zip(widths, scales):
        pieces.append((off, off + w, s))
        off += w
    assert off == w_bf16.shape[1]
    return pl.pallas_call(
        functools.partial(_proj_kernel, pieces=tuple(pieces)),
        grid=(T // tm,),
        in_specs=[pl.BlockSpec((tm, D), lambda i: (i, 0)),
                  pl.BlockSpec(w_bf16.shape, lambda i: (0, 0))],
        out_specs=[pl.BlockSpec((tm, w), lambda i: (i, 0)) for w in widths],
        out_shape=[jax.ShapeDtypeStruct((T, w), BF16) for w in widths],
        compiler_params=_cparams(("parallel",)),
        name="input_projection",
    )(x2d, w_bf16)


def _diff_kernel(lam_ref, q_ref, k_ref, v_ref, bias_ref, g_ref, o_ref, m_ref, l_ref, acc_ref, s_ref,
                 *, tq, tk, n_near, out_scale):
    i = pl.program_id(2)
    q = q_ref[...]
    lane = lax.broadcasted_iota(I32, q.shape, 1)
    zero = jnp.zeros_like(q)
    qq = jnp.concatenate([jnp.where(lane < DIFF_HD, q, zero), jnp.where(lane >= DIFF_HD, q, zero)], axis=0)

    m_ref[...] = jnp.full(m_ref.shape, MASK_VALUE, F32)
    l_ref[...] = jnp.zeros(l_ref.shape, F32)
    acc_ref[...] = jnp.zeros(acc_ref.shape, F32)

    def run(tiles):
        for n, (j, _) in enumerate(tiles):
            k = k_ref[pl.ds(pl.multiple_of(j * tk, tk), tk), :]
            for mi in range(2):
                s_ref[2 * n + mi] = lax.dot_general(k, qq[mi * tq:(mi + 1) * tq], (((1,), (1,)), ((), ())),
                                                    preferred_element_type=F32)
        for n, (j, bias) in enumerate(tiles):
            v = v_ref[pl.ds(pl.multiple_of(j * tk, tk), tk), :]
            for mi in range(2):
                cols = slice(mi * tq, (mi + 1) * tq)
                s = s_ref[2 * n + mi]
                if bias is not None:
                    s = s + bias
                m_prev = m_ref[:, cols]
                m_new = jnp.maximum(m_prev, jnp.max(s, axis=0, keepdims=True))
                alpha = jnp.exp2(m_prev - m_new)
                p = jnp.exp2(s - m_new)
                l_ref[:, cols] = alpha * l_ref[:, cols] + jnp.sum(p, axis=0, keepdims=True)
                pv = lax.dot_general(v, p.astype(BF16), (((0,), (0,)), ((), ())), preferred_element_type=F32)
                acc_ref[:, cols] = alpha * acc_ref[:, cols] + pv
                m_ref[:, cols] = m_new

    def near_tile(delta):
        return jnp.maximum(i - delta, 0), bias_ref[0, jnp.where(i >= delta, delta, n_near)]

    for a in range(0, n_near, 2):
        run([near_tile(delta) for delta in range(a, min(a + 2, n_near))])

    n_far = jnp.maximum(i - (n_near - 1), 0)

    n_group = FAR_TILES_PER_STEP
    n_loop = n_far // n_group

    def far_body(jj, carry):
        run([(n_group * jj + n, None) for n in range(n_group)])
        return carry

    lax.fori_loop(0, n_loop, far_body, 0)

    done = n_loop * n_group
    size = n_group // 2
    while size >= 1:
        take = ((n_far - done) & size) != 0

        @pl.when(take)
        def _(done=done, size=size):
            run([(done + n, None) for n in range(size)])

        done = done + jnp.where(take, size, 0)
        size //= 2

    o = acc_ref[...] / l_ref[...]
    out = o[:, :tq] - lam_ref[0] * o[:, tq:]
    out = out * lax.rsqrt(jnp.mean(out * out, axis=0, keepdims=True) + SUBLN_EPS)
    o_ref[...] = (out.T * g_ref[...] * out_scale).astype(o_ref.dtype)


def _toeplitz_tile(rel, offset, tk, tq):
    n_heads, n = rel.shape
    span = tk + tq - 1
    pad_r = max(offset + span - (tk - 1) - n, 0)
    relp = jnp.pad(rel, ((0, 0), (tk - 1, pad_r)), constant_values=MASK_VALUE)
    v = relp[:, offset:offset + span]
    v = jnp.pad(v, ((0, 0), (0, 1)))
    m = jnp.tile(v, (1, tk))[:, :tk * span].reshape(n_heads, tk, span)
    return m[:, :, tk - 1:tk - 1 + tq]


def _near_tile_count(t):
    n = np.arange(1, 4 * REL_MAX_DIST, dtype=np.float32)
    large = 16 + (np.log(n / 16) / math.log(REL_MAX_DIST / 16) * 16).astype(np.int32)
    first_const = int(np.max(np.nonzero(large < REL_BUCKETS - 1)[0])) + 2
    first_const += 16
    delta = 0
    while (delta - 1) * t + 1 < first_const:
        delta += 1
    return delta


def _diff_attention(dq, dk, dv, lam, lam_init, bias_dist, subln_g, B, S, tq=512):
    T = B * S
    tk = tq
    nq = S // tq
    n_near = min(_near_tile_count(tq), nq)
    far = bias_dist[S - 1] if n_near < nq else jnp.zeros_like(bias_dist[0])
    rel = ((bias_dist[:min(S, n_near * tq)] - far) * LOG2_E).T
    tiles = [_toeplitz_tile(rel, delta * tq, tk, tq) for delta in range(n_near)]
    tiles.append(jnp.full_like(tiles[0], MASK_VALUE))
    bias_tiles = jnp.stack(tiles, axis=1).astype(F32)

    kern = functools.partial(_diff_kernel, tq=tq, tk=tk, n_near=n_near, out_scale=1.0 - lam_init)
    return pl.pallas_call(
        kern,
        grid=(B, DIFF_HEADS, nq),
        in_specs=[pl.BlockSpec(memory_space=pltpu.SMEM),
                  pl.BlockSpec((tq, LANES), lambda b, h, i: (b * nq + i, h)),
                  pl.BlockSpec((S, LANES), lambda b, h, i: (b, h)),
                  pl.BlockSpec((S, LANES), lambda b, h, i: (b, h)),
                  pl.BlockSpec((1, n_near + 1, tk, tq), lambda b, h, i: (h, 0, 0, 0)),
                  pl.BlockSpec((1, DIFF_VD), lambda b, h, i: (0, 0))],
        out_specs=pl.BlockSpec((tq, LANES), lambda b, h, i: (b * nq + i, h)),
        out_shape=jax.ShapeDtypeStruct((T, DIFF_V_W), BF16),
        scratch_shapes=[pltpu.VMEM((1, 2 * tq), F32), pltpu.VMEM((1, 2 * tq), F32),
                        pltpu.VMEM((DIFF_VD, 2 * tq), F32), pltpu.VMEM((2 * FAR_TILES_PER_STEP, tk, tq), F32)],
        compiler_params=_cparams(("parallel", "parallel", "parallel")),
        name="diff_attention",
    )(lam.reshape(1), dq, dk, dv, bias_tiles, subln_g.reshape(1, DIFF_VD))


def _dil_blocks(t):
    return tuple((g, delta) for g, (window, _) in enumerate(DIL_PATTERNS) for delta in range(window // t + 1))


def _dil_kernel(*refs, t, blocks):
    nb = len(blocks)
    q_ref = refs[0]
    k_refs = refs[1:1 + nb]
    v_refs = refs[1 + nb:1 + 2 * nb]
    bias_ref = refs[1 + 2 * nb]
    o_ref = refs[2 + 2 * nb]
    kcat_ref, vcat_ref = refs[3 + 2 * nb:]
    i = pl.program_id(1)

    lane = lax.broadcasted_iota(I32, (t, DIL_GROUP_W), 1)
    head_masks = [(lane >= h * DIL_HD) & (lane < (h + 1) * DIL_HD) for h in range(DIL_HEADS)]

    scores = []
    for g in range(N_DIL_GROUPS):
        idx = [bi for bi, (bg, _) in enumerate(blocks) if bg == g]
        lo, n = idx[0] * t, len(idx) * t
        for bi in idx:
            kcat_ref[bi * t:(bi + 1) * t, :] = k_refs[bi][...]
            vcat_ref[bi * t:(bi + 1) * t, :] = v_refs[bi][...]
        q = q_ref[:, g * DIL_GROUP_W:(g + 1) * DIL_GROUP_W]
        zero = jnp.zeros_like(q)
        qs = jnp.concatenate([jnp.where(hm, q, zero) for hm in head_masks], axis=0)
        s = lax.dot_general(kcat_ref[lo:lo + n, :], qs, (((1,), (1,)), ((), ())),
                            preferred_element_type=F32) + bias_ref[lo:lo + n, :]
        delta_of_row = lax.broadcasted_iota(I32, s.shape, 0) // t
        scores.append(jnp.where(delta_of_row <= i, s, MASK_VALUE))

    m = functools.reduce(jnp.maximum, [jnp.max(s, axis=0, keepdims=True) for s in scores])
    l = jnp.zeros_like(m)
    acc = jnp.zeros((DIL_GROUP_W, DIL_HEADS * t), F32)
    for g, s in enumerate(scores):
        idx = [bi for bi, (bg, _) in enumerate(blocks) if bg == g]
        lo, n = idx[0] * t, len(idx) * t
        p = jnp.exp2(s - m)
        l = l + jnp.sum(p, axis=0, keepdims=True)
        acc = acc + lax.dot_general(vcat_ref[lo:lo + n, :], p.astype(BF16), (((0,), (0,)), ((), ())),
                                    preferred_element_type=F32)
    o = acc / l
    y_t = jnp.concatenate([o[h * DIL_HD:(h + 1) * DIL_HD, h * t:(h + 1) * t] for h in range(DIL_HEADS)], axis=0)
    o_ref[...] = y_t.T.astype(o_ref.dtype)


def _dilated_attention(lq, lk, lv, rel_bias_dil, B, S, t=128):
    T = B * S
    nq = S // t
    blocks = _dil_blocks(t)
    nb = len(blocks)
    max_w = max(w for w, _ in DIL_PATTERNS)
    bd = jnp.take(rel_bias_dil, _rel_bucket(jnp.arange(max_w + 1)), axis=0).astype(F32).T * LOG2_E
    dists = jnp.arange(max_w + 1)
    tiles = []
    for g, delta in blocks:
        window, dil = DIL_PATTERNS[g]
        rel = jnp.where((dists % dil == 0)[None, :window + 1],
                        bd[g * DIL_HEADS:(g + 1) * DIL_HEADS, :window + 1], MASK_VALUE)
        tile = _toeplitz_tile(rel, delta * t, t, t)
        tiles.append(jnp.moveaxis(tile, 0, 1).reshape(t, DIL_HEADS * t))
    bias_tiles = jnp.concatenate(tiles, axis=0)

    def kv_spec(g, delta):
        return pl.BlockSpec((t, DIL_GROUP_W), lambda b, i: (b * nq + jnp.maximum(i - delta, 0), g))

    kv_specs = [kv_spec(g, delta) for g, delta in blocks]
    return pl.pallas_call(
        functools.partial(_dil_kernel, t=t, blocks=blocks),
        grid=(B, nq),
        in_specs=[pl.BlockSpec((t, DIL_W), lambda b, i: (b * nq + i, 0))] + kv_specs + kv_specs
                 + [pl.BlockSpec(bias_tiles.shape, lambda b, i: (0, 0))],
        out_specs=pl.BlockSpec((t, DIL_GROUP_W), lambda b, i: (b * nq + i, 0)),
        out_shape=jax.ShapeDtypeStruct((T, DIL_GROUP_W), BF16),
        scratch_shapes=[pltpu.VMEM((nb * t, DIL_GROUP_W), BF16), pltpu.VMEM((nb * t, DIL_GROUP_W), BF16)],
        compiler_params=_cparams(("parallel", "parallel")),
        name="dilated_attention",
    )(lq, *([lk] * nb), *([lv] * nb), bias_tiles)


def _layer_norm(y, g, b):
    mu = jnp.mean(y, axis=-1, keepdims=True)
    var = jnp.mean(jnp.square(y - mu), axis=-1, keepdims=True)
    return (y - mu) * lax.rsqrt(var + LN_EPS) * g + b


def _post_kernel(x_ref, ya_ref, yb_ref, ga_ref, gb_ref, wd_ref, wl_ref, wo_ref, g1_ref, b1_ref, wq_ref,
                 x1_ref, pq_ref):
    a = jnp.dot(ya_ref[...], wd_ref[...], preferred_element_type=F32)
    b = jnp.dot(yb_ref[...], wl_ref[...], preferred_element_type=F32)
    mix = jax.nn.sigmoid(ga_ref[...].astype(F32)) * a + jax.nn.sigmoid(gb_ref[...].astype(F32)) * b
    mixed = jnp.dot(mix.astype(BF16), wo_ref[...], preferred_element_type=F32)
    x1 = _layer_norm(ALPHA * x_ref[...] + mixed, g1_ref[...], b1_ref[...])
    x1_ref[...] = x1
    pq_ref[...] = jnp.dot(x1.astype(BF16), wq_ref[...], preferred_element_type=F32).astype(BF16)


def _post_attention(x2d, ya, yb, ga, gb, wd, wl, wo, g1, b1, wq, tm=256):
    T, D = x2d.shape
    NQ = wq.shape[1]

    def row(w):
        return pl.BlockSpec((tm, w), lambda i: (i, 0))

    def full(a):
        return pl.BlockSpec(a.shape, lambda i: (0,) * a.ndim)

    return pl.pallas_call(
        _post_kernel,
        grid=(T // tm,),
        in_specs=[row(D), row(ya.shape[1]), row(yb.shape[1]), row(D), row(D),
                  full(wd), full(wl), full(wo), full(g1), full(b1), full(wq)],
        out_specs=[row(D), row(NQ)],
        out_shape=[jax.ShapeDtypeStruct((T, D), F32), jax.ShapeDtypeStruct((T, NQ), BF16)],
        compiler_params=_cparams(("parallel",)),
        name="post_attention",
    )(x2d, ya, yb, ga, gb, wd, wl, wo, g1, b1, wq)


def _top_rows(s, k, payload=None):
    n = s.shape[0]
    row = lax.broadcasted_iota(I32, s.shape, 0)
    vals, picks = [], []
    for _ in range(k):
        m = jnp.max(s, axis=0, keepdims=True)
        idx = jnp.min(jnp.where(s == m, row, n), axis=0, keepdims=True)
        hit = row == idx
        if payload is not None:
            idx = jnp.max(jnp.where(hit, payload, -1), axis=0, keepdims=True)
        s = jnp.where(hit, -jnp.inf, s)
        vals.append(m)
        picks.append(idx)
    return jnp.concatenate(vals, axis=0), jnp.concatenate(picks, axis=0)


def _candidate_ranks():
    return [(a, PEER_TOPK // (a + 1)) for a in range(PEER_TOPK)]


def _retrieve_kernel(pq_ref, keys_ref, row_ref, sh_ref, g_ref):
    e_rows, g_rows = [], []
    for h in range(PEER_HEADS):
        tops = []
        for c in range(2):
            col = (h * 2 + c) * PEER_DKEY
            qh = pq_ref[:, col:col + PEER_DKEY]
            s = lax.dot_general(keys_ref[h, c], qh, (((1,), (1,)), ((), ())),
                                preferred_element_type=F32)
            tops.append(_top_rows(s, PEER_TOPK))
        (s0, i0), (s1, i1) = tops
        ranks = _candidate_ranks()
        n_cand = sum(nb for _, nb in ranks)
        pad = -n_cand % SUBLANES
        cand = jnp.concatenate([s0[a:a + 1] + s1[:nb] for a, nb in ranks]
                               + [jnp.full((pad, s0.shape[1]), -jnp.inf, F32)], axis=0)
        cand_idx = jnp.concatenate([i0[a:a + 1] * PEER_NKEYS + i1[:nb] for a, nb in ranks]
                                   + [jnp.zeros((pad, s0.shape[1]), I32)], axis=0)
        best, eidx = _top_rows(cand, PEER_TOPK, payload=cand_idx)
        e = jnp.exp(best - jnp.max(best, axis=0, keepdims=True))
        g_rows.append(e / jnp.sum(e, axis=0, keepdims=True))
        e_rows.append(eidx)
    eidx = jnp.concatenate(e_rows, axis=0)
    row_ref[...] = ((eidx >> 1) * SUBLANES).T
    sh_ref[...] = (((eidx & 1) ^ 1) << 4).T
    g_ref[...] = jnp.concatenate(g_rows, axis=0).T


def _peer_retrieve(pq, keys_bf16, tt=128):
    T = pq.shape[0]
    out = pl.BlockSpec((tt, PEER_PAIRS), lambda i: (i, 0))
    return pl.pallas_call(
        _retrieve_kernel,
        grid=(T // tt,),
        in_specs=[pl.BlockSpec((tt, pq.shape[1]), lambda i: (i, 0)),
                  pl.BlockSpec(keys_bf16.shape, lambda i: (0, 0, 0, 0))],
        out_specs=[out, out, out],
        out_shape=[jax.ShapeDtypeStruct((T, PEER_PAIRS), I32), jax.ShapeDtypeStruct((T, PEER_PAIRS), I32),
                   jax.ShapeDtypeStruct((T, PEER_PAIRS), F32)],
        compiler_params=_cparams(("parallel",)),
        name="peer_retrieve",
    )(pq, keys_bf16)


def _pack_kernel(w_ref, o_ref):
    n_tiles = o_ref.shape[0] // SUBLANES
    d = SUBLANES * LANES

    def bf16_bits_high(x):
        return pltpu.bitcast(x.astype(BF16).astype(F32), I32)

    word = lax.shift_right_logical(bf16_bits_high(w_ref[:, :d]), 16) | bf16_bits_high(w_ref[:, d:])
    for c in range(SUBLANES):
        o_ref[pl.ds(c, n_tiles, stride=SUBLANES), :] = word[:, c * LANES:(c + 1) * LANES]


def _pack_table(w, tiles_per_step=256):
    n, d = w.shape
    assert d == SUBLANES * LANES
    return pl.pallas_call(
        _pack_kernel,
        grid=(n // (2 * tiles_per_step),),
        in_specs=[pl.BlockSpec((tiles_per_step, 2 * d), lambda i: (i, 0))],
        out_specs=pl.BlockSpec((tiles_per_step * SUBLANES, LANES), lambda i: (i, 0)),
        out_shape=jax.ShapeDtypeStruct((n // 2 * SUBLANES, LANES), I32),
        compiler_params=_cparams(("parallel",)),
        name="pack_table",
    )(w.reshape(n // 2, 2 * d))


def _expert_tile(tab_ref, row, shift):
    tile = tab_ref[pl.ds(pl.multiple_of(row, SUBLANES), SUBLANES), :]
    return pltpu.bitcast((tile << shift) & HI16, F32)


def _token_tile(ref, t):
    sub = lax.broadcasted_iota(I32, (SUBLANES, LANES), 0)
    full = ref[pl.ds(t, 1), :]
    tile = None
    for c in range(SUBLANES):
        row = jnp.broadcast_to(full[:, c * LANES:(c + 1) * LANES], (SUBLANES, LANES))
        tile = row if tile is None else jnp.where(sub == c, row, tile)
    return tile


def _lane_rows(ref, t):
    return jnp.broadcast_to(ref[pl.ds(t, 1), :], (PEER_PAIRS, LANES)).T


def _staged_token_loop(rows_ref, idx_ref, sems, tt, prep, compute, finish):
    n_chunks = tt // STAGE_TOKENS

    def stage(chunk, slot):
        src = rows_ref.at[pl.ds(pl.multiple_of(chunk * STAGE_TOKENS, STAGE_TOKENS), STAGE_TOKENS), :]
        return pltpu.make_async_copy(src, idx_ref.at[slot], sems.at[slot])

    stage(0, 0).start()
    stage(1, 1).start()
    prep(0, 0)

    def body(i, carry):
        for slot in range(2):
            chunk = 2 * i + slot
            stage(chunk, slot).wait()
            for g in range(STAGE_TOKENS):
                t = chunk * STAGE_TOKENS + g
                vslot = g % 2
                prep(jnp.minimum(t + 1, tt - 1), 1 - vslot)
                finish(jnp.maximum(t - 1, 0), 1 - vslot)
                compute(t, vslot, functools.partial(lambda s, gg, k: idx_ref[s, gg, k], slot, g))
            stage(jnp.minimum(chunk + 2, n_chunks - 1), slot).start()
        return carry

    lax.fori_loop(0, n_chunks // 2, body, 0)
    stage(n_chunks - 1, 0).wait()
    stage(n_chunks - 1, 1).wait()
    finish(tt - 1, (tt - 1) % 2)


def _peer_u_kernel(row_ref, sh_ref, x_ref, tab_ref, h_ref, shb_ref, prod_ref, part_ref, idx_ref, sems):
    tt = x_ref.shape[0]
    tile_rows = SUBLANES * SUBLANES

    def prep(t, slot):
        shb_ref[slot] = _lane_rows(sh_ref, t)

    def compute(t, slot, row_of):
        x = _token_tile(x_ref, t)
        for g in range(PEER_PAIRS // SUBLANES):
            base = g * tile_rows
            for j in range(SUBLANES):
                k = g * SUBLANES + j
                f = _expert_tile(tab_ref, row_of(k), shb_ref[slot, k:k + 1, :])
                pos = j // 2 + (SUBLANES // 2) * (j % 2)
                prod_ref[slot, base + pos * SUBLANES:base + (pos + 1) * SUBLANES, :] = f * x
            quads = []
            for half in range(2):
                hb = base + half * (tile_rows // 2)
                r4 = prod_ref[slot, pl.ds(hb, SUBLANES, stride=HALF_SUBLANES), :]
                for c in range(1, HALF_SUBLANES):
                    r4 = r4 + prod_ref[slot, pl.ds(hb + c, SUBLANES, stride=HALF_SUBLANES), :]
                quads.append(r4 + pltpu.roll(r4, SUBLANES - 1, axis=0))
            even = lax.broadcasted_iota(I32, (SUBLANES, LANES), 0) % 2 == 0
            part_ref[slot, g * SUBLANES:(g + 1) * SUBLANES, :] = jnp.where(
                even, quads[0], pltpu.roll(quads[1], 1, axis=0))

    def finish(t, slot):
        h_ref[pl.ds(t, 1), :] = jnp.sum(part_ref[slot].T, axis=0, keepdims=True)

    part_ref[1] = jnp.zeros(part_ref.shape[1:], F32)
    _staged_token_loop(row_ref, idx_ref, sems, tt, prep, compute, finish)


def _stage_scratch():
    return [pltpu.SMEM((2, STAGE_TOKENS, PEER_PAIRS), I32), pltpu.SemaphoreType.DMA((2,))]


def _peer_u(rows, shifts, x1, utab, tt=128):
    T, D = x1.shape
    pair_spec = pl.BlockSpec((tt, PEER_PAIRS), lambda i: (i, 0))
    return pl.pallas_call(
        _peer_u_kernel,
        grid=(T // tt,),
        in_specs=[pair_spec, pair_spec,
                  pl.BlockSpec((tt, D), lambda i: (i, 0)),
                  pl.BlockSpec(memory_space=pltpu.VMEM)],
        out_specs=pair_spec,
        out_shape=jax.ShapeDtypeStruct((T, PEER_PAIRS), F32),
        scratch_shapes=[pltpu.VMEM((2, PEER_PAIRS, LANES), I32),
                        pltpu.VMEM((2, PEER_PAIRS * SUBLANES, LANES), F32),
                        pltpu.VMEM((2, PEER_PAIRS, LANES), F32)] + _stage_scratch(),
        compiler_params=_cparams(("arbitrary",)),
        name="peer_u",
    )(rows, shifts, x1, utab)


def _peer_v_kernel(row_ref, sh_ref, g_ref, h_ref, tab_ref, o_ref, w_ref, shb_ref, wb_ref, idx_ref, sems):
    tt = o_ref.shape[0]
    n_acc = 4
    h = h_ref[...]
    w_ref[...] = g_ref[...] * (0.5 * h * (1.0 + lax.erf(h * (2.0 ** -0.5))))

    def prep(t, slot):
        shb_ref[slot] = _lane_rows(sh_ref, t)
        wb_ref[slot] = _lane_rows(w_ref, t)

    def compute(t, slot, row_of):
        accs = [None] * n_acc
        for k in range(PEER_PAIRS):
            f = _expert_tile(tab_ref, row_of(k), shb_ref[slot, k:k + 1, :])
            term = wb_ref[slot, k:k + 1, :] * f
            accs[k % n_acc] = term if accs[k % n_acc] is None else accs[k % n_acc] + term
        o_ref[t] = functools.reduce(lambda a, b: a + b, accs)

    def finish(t, slot):
        pass

    _staged_token_loop(row_ref, idx_ref, sems, tt, prep, compute, finish)


def _peer_v(rows, shifts, gates, h, vtab, tt=128):
    T = rows.shape[0]
    pair_spec = pl.BlockSpec((tt, PEER_PAIRS), lambda i: (i, 0))
    return pl.pallas_call(
        _peer_v_kernel,
        grid=(T // tt,),
        in_specs=[pair_spec, pair_spec, pair_spec, pair_spec,
                  pl.BlockSpec(memory_space=pltpu.VMEM)],
        out_specs=pl.BlockSpec((tt, SUBLANES, LANES), lambda i: (i, 0, 0)),
        out_shape=jax.ShapeDtypeStruct((T, SUBLANES, LANES), F32),
        scratch_shapes=[pltpu.VMEM((tt, PEER_PAIRS), F32),
                        pltpu.VMEM((2, PEER_PAIRS, LANES), I32),
                        pltpu.VMEM((2, PEER_PAIRS, LANES), F32)] + _stage_scratch(),
        compiler_params=_cparams(("arbitrary",)),
        name="peer_v",
    )(rows, shifts, gates, h, vtab)


def _final_kernel(x1_ref, f_ref, g_ref, b_ref, o_ref):
    o_ref[...] = _layer_norm(ALPHA * x1_ref[...] + f_ref[...], g_ref[...], b_ref[...])


def _final_norm(x1, ffn, g, b, tm=512):
    T, D = x1.shape
    return pl.pallas_call(
        _final_kernel,
        grid=(T // tm,),
        in_specs=[pl.BlockSpec((tm, D), lambda i: (i, 0)), pl.BlockSpec((tm, D), lambda i: (i, 0)),
                  pl.BlockSpec((1, D), lambda i: (0, 0)), pl.BlockSpec((1, D), lambda i: (0, 0))],
        out_specs=pl.BlockSpec((tm, D), lambda i: (i, 0)),
        out_shape=jax.ShapeDtypeStruct((T, D), F32),
        compiler_params=_cparams(("parallel",)),
        name="final_norm",
    )(x1, ffn, g, b)


def kernel(x, w_in, diff_lambda, diff_subln_g, w_branch_diff, w_branch_dil, w_out, ln1_g, ln1_b,
           peer_wq, peer_keys, peer_u, peer_v, ln2_g, ln2_b, rel_bias):
    B, S, D = x.shape
    T = B * S
    assert w_in.shape[0] == DEPTH
    bias_dist = jnp.take(rel_bias, _rel_bucket(jnp.arange(S)), axis=0)
    x2d = x.reshape(T, D)
    for l in range(DEPTH):
        lam_init = 0.8 - 0.6 * math.exp(-0.3 * l)
        lp = diff_lambda[l].astype(F32)
        lam = jnp.exp(jnp.sum(lp[0] * lp[1])) - jnp.exp(jnp.sum(lp[2] * lp[3])) + lam_init

        dq, dk, dv, lq, lk, lv, ga, gb = _input_projection(x2d, w_in[l].astype(BF16))
        ya = _diff_attention(dq, dk, dv, lam, lam_init, bias_dist[:, :DIFF_HEADS], diff_subln_g[l], B, S)
        yb = _dilated_attention(lq, lk, lv, rel_bias[:, DIFF_HEADS:], B, S)
        x1, pq = _post_attention(
            x2d, ya, yb, ga, gb, w_branch_diff[l].astype(BF16), w_branch_dil[l].astype(BF16),
            w_out[l].astype(BF16), ln1_g[l].reshape(1, D), ln1_b[l].reshape(1, D), peer_wq[l].astype(BF16))

        rows, shifts, gates = _peer_retrieve(pq, peer_keys[l].astype(BF16))
        h = _peer_u(rows, shifts, x1, _pack_table(peer_u[l]))
        ffn = _peer_v(rows, shifts, gates, h, _pack_table(peer_v[l]))
        x2d = _final_norm(x1, ffn.reshape(T, D), ln2_g[l].reshape(1, D), ln2_b[l].reshape(1, D))
    return x2d.reshape(B, S, D)
```

```python
import functools
import math

import numpy as np
import jax
import jax.numpy as jnp
from jax import lax
from jax.experimental import pallas as pl
from jax.experimental.pallas import tpu as pltpu

F32 = jnp.float32
BF16 = jnp.bfloat16
I32 = jnp.int32

DEPTH = 1
DIFF_HEADS = 4
DIFF_HD = 64
DIFF_VD = 2 * DIFF_HD
DIL_PATTERNS = ((128, 1), (512, 4), (2048, 16))
N_DIL_GROUPS = len(DIL_PATTERNS)
DIL_HEADS = 4
DIL_HD = 64
REL_BUCKETS = 32
REL_MAX_DIST = 2048
PEER_HEADS = 8
PEER_NKEYS = 128
PEER_TOPK = 16
PEER_DKEY = 128
ALPHA = (2.0 * DEPTH) ** 0.25
LN_EPS = 1e-5
SUBLN_EPS = 1e-5

DIFF_QK_W = DIFF_HEADS * 2 * DIFF_HD
DIFF_V_W = DIFF_HEADS * DIFF_VD
DIL_W = N_DIL_GROUPS * DIL_HEADS * DIL_HD
DIL_GROUP_W = DIL_HEADS * DIL_HD

LANES = 128
SUBLANES = 8
HALF_SUBLANES = SUBLANES // 2
VMEM_LIMIT = 56 * 1024 * 1024

MASK_VALUE = -1e30

PEER_PAIRS = PEER_HEADS * PEER_TOPK
STAGE_TOKENS = 16
HI16 = -65536
LOG2_E = math.log2(math.e)
FAR_TILES_PER_STEP = 8


def _cparams(sem):
    return pltpu.CompilerParams(dimension_semantics=sem, vmem_limit_bytes=VMEM_LIMIT)


def _rel_bucket(dist):
    n = jnp.maximum(dist, 0)
    max_exact = REL_BUCKETS // 2
    nf = jnp.maximum(n, 1).astype(F32)
    large = max_exact + (jnp.log(nf / max_exact) / math.log(REL_MAX_DIST / max_exact)
                         * (REL_BUCKETS - max_exact)).astype(I32)
    large = jnp.minimum(large, REL_BUCKETS - 1)
    return jnp.where(n < max_exact, n, large)


def _proj_kernel(x_ref, w_ref, *out_refs, pieces):
    xb = x_ref[...].astype(BF16)
    for o_ref, (lo, hi, scale) in zip(out_refs, pieces):
        acc = jnp.dot(xb, w_ref[:, lo:hi], preferred_element_type=F32)
        if scale != 1.0:
            acc = acc * scale
        o_ref[...] = acc.astype(BF16)


def _input_projection(x2d, w_bf16, tm=256):
    T, D = x2d.shape
    widths = (DIFF_QK_W, DIFF_QK_W, DIFF_V_W, DIL_W, DIL_W, DIL_W, D, D)
    scales = (DIFF_HD ** -0.5 * LOG2_E, 1.0, 1.0, DIL_HD ** -0.5 * LOG2_E, 1.0, 1.0, 1.0, 1.0)
    pieces, off = [], 0
    for w, s in zip(widths, scales):
        pieces.append((off, off + w, s))
        off += w
    assert off == w_bf16.shape[1]
    return pl.pallas_call(
        functools.partial(_proj_kernel, pieces=tuple(pieces)),
        grid=(T // tm,),
        in_specs=[pl.BlockSpec((tm, D), lambda i: (i, 0)),
                  pl.BlockSpec(w_bf16.shape, lambda i: (0, 0))],
        out_specs=[pl.BlockSpec((tm, w), lambda i: (i, 0)) for w in widths],
        out_shape=[jax.ShapeDtypeStruct((T, w), BF16) for w in widths],
        compiler_params=_cparams(("parallel",)),
        name="input_projection",
    )(x2d, w_bf16)


def _diff_kernel(lam_ref, q_ref, k_ref, v_ref, bias_ref, g_ref, o_ref, m_ref, l_ref, acc_ref, s_ref,
                 *, tq, tk, n_near, out_scale):
    i = pl.program_id(2)
    q = q_ref[...]
    lane = lax.broadcasted_iota(I32, q.shape, 1)
    zero = jnp.zeros_like(q)
    qq = jnp.concatenate([jnp.where(lane < DIFF_HD, q, zero), jnp.where(lane >= DIFF_HD, q, zero)], axis=0)

    m_ref[...] = jnp.full(m_ref.shape, MASK_VALUE, F32)
    l_ref[...] = jnp.zeros(l_ref.shape, F32)
    acc_ref[...] = jnp.zeros(acc_ref.shape, F32)

    def run(tiles):
        for n, (j, _) in enumerate(tiles):
            k = k_ref[pl.ds(pl.multiple_of(j * tk, tk), tk), :]
            for mi in range(2):
                s_ref[2 * n + mi] = lax.dot_general(k, qq[mi * tq:(mi + 1) * tq], (((1,), (1,)), ((), ())),
                                                    preferred_element_type=F32)
        for n, (j, bias) in enumerate(tiles):
            v = v_ref[pl.ds(pl.multiple_of(j * tk, tk), tk), :]
            for mi in range(2):
                cols = slice(mi * tq, (mi + 1) * tq)
                s = s_ref[2 * n + mi]
                if bias is not None:
                    s = s + bias
                m_prev = m_ref[:, cols]
                m_new = jnp.maximum(m_prev, jnp.max(s, axis=0, keepdims=True))
                alpha = jnp.exp2(m_prev - m_new)
                p = jnp.exp2(s - m_new)
                l_ref[:, cols] = alpha * l_ref[:, cols] + jnp.sum(p, axis=0, keepdims=True)
                pv = lax.dot_general(v, p.astype(BF16), (((0,), (0,)), ((), ())), preferred_element_type=F32)
                acc_ref[:, cols] = alpha * acc_ref[:, cols] + pv
                m_ref[:, cols] = m_new

    def near_tile(delta):
        return jnp.maximum(i - delta, 0), bias_ref[0, jnp.where(i >= delta, delta, n_near)]

    for a in range(0, n_near, 2):
        run([near_tile(delta) for delta in range(a, min(a + 2, n_near))])

    n_far = jnp.maximum(i - (n_near - 1), 0)

    n_group = FAR_TILES_PER_STEP
    n_loop = n_far // n_group

    def far_body(jj, carry):
        run([(n_group * jj + n, None) for n in range(n_group)])
        return carry

    lax.fori_loop(0, n_loop, far_body, 0)

    done = n_loop * n_group
    size = n_group // 2
    while size >= 1:
        take = ((n_far - done) & size) != 0

        @pl.when(take)
        def _(done=done, size=size):
            run([(done + n, None) for n in range(size)])

        done = done + jnp.where(take, size, 0)
        size //= 2

    o = acc_ref[...] / l_ref[...]
    out = o[:, :tq] - lam_ref[0] * o[:, tq:]
    out = out * lax.rsqrt(jnp.mean(out * out, axis=0, keepdims=True) + SUBLN_EPS)
    o_ref[...] = (out.T * g_ref[...] * out_scale).astype(o_ref.dtype)


def _toeplitz_tile(rel, offset, tk, tq):
    n_heads, n = rel.shape
    span = tk + tq - 1
    pad_r = max(offset + span - (tk - 1) - n, 0)
    relp = jnp.pad(rel, ((0, 0), (tk - 1, pad_r)), constant_values=MASK_VALUE)
    v = relp[:, offset:offset + span]
    v = jnp.pad(v, ((0, 0), (0, 1)))
    m = jnp.tile(v, (1, tk))[:, :tk * span].reshape(n_heads, tk, span)
    return m[:, :, tk - 1:tk - 1 + tq]


def _near_tile_count(t):
    n = np.arange(1, 4 * REL_MAX_DIST, dtype=np.float32)
    large = 16 + (np.log(n / 16) / math.log(REL_MAX_DIST / 16) * 16).astype(np.int32)
    first_const = int(np.max(np.nonzero(large < REL_BUCKETS - 1)[0])) + 2
    first_const += 16
    delta = 0
    while (delta - 1) * t + 1 < first_const:
        delta += 1
    return delta


def _diff_attention(dq, dk, dv, lam, lam_init, rel_bias, subln_g, B, S, tq=512):
    T = B * S
    tk = tq
    nq = S // tq
    n_near = min(_near_tile_count(tq), nq)
    bias_dist = jnp.take(rel_bias, _rel_bucket(jnp.arange(min(S, n_near * tq))), axis=0)
    far = rel_bias[_rel_bucket(jnp.asarray(S - 1))] if n_near < nq else jnp.zeros_like(rel_bias[0])
    rel = ((bias_dist - far) * LOG2_E).T
    tiles = [_toeplitz_tile(rel, delta * tq, tk, tq) for delta in range(n_near)]
    tiles.append(jnp.full_like(tiles[0], MASK_VALUE))
    bias_tiles = jnp.stack(tiles, axis=1).astype(F32)

    kern = functools.partial(_diff_kernel, tq=tq, tk=tk, n_near=n_near, out_scale=1.0 - lam_init)
    return pl.pallas_call(
        kern,
        grid=(B, DIFF_HEADS, nq),
        in_specs=[pl.BlockSpec(memory_space=pltpu.SMEM),
                  pl.BlockSpec((tq, LANES), lambda b, h, i: (b * nq + i, h)),
                  pl.BlockSpec((S, LANES), lambda b, h, i: (b, h)),
                  pl.BlockSpec((S, LANES), lambda b, h, i: (b, h)),
                  pl.BlockSpec((1, n_near + 1, tk, tq), lambda b, h, i: (h, 0, 0, 0)),
                  pl.BlockSpec((1, DIFF_VD), lambda b, h, i: (0, 0))],
        out_specs=pl.BlockSpec((tq, LANES), lambda b, h, i: (b * nq + i, h)),
        out_shape=jax.ShapeDtypeStruct((T, DIFF_V_W), BF16),
        scratch_shapes=[pltpu.VMEM((1, 2 * tq), F32), pltpu.VMEM((1, 2 * tq), F32),
                        pltpu.VMEM((DIFF_VD, 2 * tq), F32), pltpu.VMEM((2 * FAR_TILES_PER_STEP, tk, tq), F32)],
        compiler_params=_cparams(("parallel", "parallel", "parallel")),
        name="diff_attention",
    )(lam.reshape(1), dq, dk, dv, bias_tiles, subln_g.reshape(1, DIFF_VD))


def _dil_blocks(t):
    return tuple((g, delta) for g, (window, _) in enumerate(DIL_PATTERNS) for delta in range(window // t + 1))


def _dil_kernel(*refs, t, blocks):
    nb = len(blocks)
    q_ref = refs[0]
    k_refs = refs[1:1 + nb]
    v_refs = refs[1 + nb:1 + 2 * nb]
    bias_ref = refs[1 + 2 * nb]
    o_ref = refs[2 + 2 * nb]
    kcat_ref, vcat_ref = refs[3 + 2 * nb:]
    i = pl.program_id(1)

    lane = lax.broadcasted_iota(I32, (t, DIL_GROUP_W), 1)
    head_masks = [(lane >= h * DIL_HD) & (lane < (h + 1) * DIL_HD) for h in range(DIL_HEADS)]

    scores = []
    for g in range(N_DIL_GROUPS):
        idx = [bi for bi, (bg, _) in enumerate(blocks) if bg == g]
        lo, n = idx[0] * t, len(idx) * t
        for bi in idx:
            kcat_ref[bi * t:(bi + 1) * t, :] = k_refs[bi][...]
            vcat_ref[bi * t:(bi + 1) * t, :] = v_refs[bi][...]
        q = q_ref[:, g * DIL_GROUP_W:(g + 1) * DIL_GROUP_W]
        zero = jnp.zeros_like(q)
        qs = jnp.concatenate([jnp.where(hm, q, zero) for hm in head_masks], axis=0)
        s = lax.dot_general(kcat_ref[lo:lo + n, :], qs, (((1,), (1,)), ((), ())),
                            preferred_element_type=F32) + bias_ref[lo:lo + n, :]
        delta_of_row = lax.broadcasted_iota(I32, s.shape, 0) // t
        scores.append(jnp.where(delta_of_row <= i, s, MASK_VALUE))

    m = functools.reduce(jnp.maximum, [jnp.max(s, axis=0, keepdims=True) for s in scores])
    l = jnp.zeros_like(m)
    acc = jnp.zeros((DIL_GROUP_W, DIL_HEADS * t), F32)
    for g, s in enumerate(scores):
        idx = [bi for bi, (bg, _) in enumerate(blocks) if bg == g]
        lo, n = idx[0] * t, len(idx) * t
        p = jnp.exp2(s - m)
        l = l + jnp.sum(p, axis=0, keepdims=True)
        acc = acc + lax.dot_general(vcat_ref[lo:lo + n, :], p.astype(BF16), (((0,), (0,)), ((), ())),
                                    preferred_element_type=F32)
    o = acc / l
    y_t = jnp.concatenate([o[h * DIL_HD:(h + 1) * DIL_HD, h * t:(h + 1) * t] for h in range(DIL_HEADS)], axis=0)
    o_ref[...] = y_t.T.astype(o_ref.dtype)


def _dilated_attention(lq, lk, lv, rel_bias_dil, B, S, t=128):
    T = B * S
    nq = S // t
    blocks = _dil_blocks(t)
    nb = len(blocks)
    max_w = max(w for w, _ in DIL_PATTERNS)
    bd = jnp.take(rel_bias_dil, _rel_bucket(jnp.arange(max_w + 1)), axis=0).astype(F32).T * LOG2_E
    dists = jnp.arange(max_w + 1)
    tiles = []
    for g, delta in blocks:
        window, dil = DIL_PATTERNS[g]
        rel = jnp.where((dists % dil == 0)[None, :window + 1],
                        bd[g * DIL_HEADS:(g + 1) * DIL_HEADS, :window + 1], MASK_VALUE)
        tile = _toeplitz_tile(rel, delta * t, t, t)
        tiles.append(jnp.moveaxis(tile, 0, 1).reshape(t, DIL_HEADS * t))
    bias_tiles = jnp.concatenate(tiles, axis=0)

    def kv_spec(g, delta):
        return pl.BlockSpec((t, DIL_GROUP_W), lambda b, i: (b * nq + jnp.maximum(i - delta, 0), g))

    kv_specs = [kv_spec(g, delta) for g, delta in blocks]
    return pl.pallas_call(
        functools.partial(_dil_kernel, t=t, blocks=blocks),
        grid=(B, nq),
        in_specs=[pl.BlockSpec((t, DIL_W), lambda b, i: (b * nq + i, 0))] + kv_specs + kv_specs
                 + [pl.BlockSpec(bias_tiles.shape, lambda b, i: (0, 0))],
        out_specs=pl.BlockSpec((t, DIL_GROUP_W), lambda b, i: (b * nq + i, 0)),
        out_shape=jax.ShapeDtypeStruct((T, DIL_GROUP_W), BF16),
        scratch_shapes=[pltpu.VMEM((nb * t, DIL_GROUP_W), BF16), pltpu.VMEM((nb * t, DIL_GROUP_W), BF16)],
        compiler_params=_cparams(("parallel", "parallel")),
        name="dilated_attention",
    )(lq, *([lk] * nb), *([lv] * nb), bias_tiles)


def _layer_norm(y, g, b):
    mu = jnp.mean(y, axis=-1, keepdims=True)
    var = jnp.mean(jnp.square(y - mu), axis=-1, keepdims=True)
    return (y - mu) * lax.rsqrt(var + LN_EPS) * g + b


def _post_kernel(x_ref, ya_ref, yb_ref, ga_ref, gb_ref, wd_ref, wl_ref, wo_ref, g1_ref, b1_ref, wq_ref,
                 x1_ref, pq_ref):
    a = jnp.dot(ya_ref[...], wd_ref[...], preferred_element_type=F32)
    b = jnp.dot(yb_ref[...], wl_ref[...], preferred_element_type=F32)
    mix = jax.nn.sigmoid(ga_ref[...].astype(F32)) * a + jax.nn.sigmoid(gb_ref[...].astype(F32)) * b
    mixed = jnp.dot(mix.astype(BF16), wo_ref[...], preferred_element_type=F32)
    x1 = _layer_norm(ALPHA * x_ref[...] + mixed, g1_ref[...], b1_ref[...])
    x1_ref[...] = x1
    pq_ref[...] = jnp.dot(x1.astype(BF16), wq_ref[...], preferred_element_type=F32).astype(BF16)


def _post_attention(x2d, ya, yb, ga, gb, wd, wl, wo, g1, b1, wq, tm=256):
    T, D = x2d.shape
    NQ = wq.shape[1]

    def row(w):
        return pl.BlockSpec((tm, w), lambda i: (i, 0))

    def full(a):
        return pl.BlockSpec(a.shape, lambda i: (0,) * a.ndim)

    return pl.pallas_call(
        _post_kernel,
        grid=(T // tm,),
        in_specs=[row(D), row(ya.shape[1]), row(yb.shape[1]), row(D), row(D),
                  full(wd), full(wl), full(wo), full(g1), full(b1), full(wq)],
        out_specs=[row(D), row(NQ)],
        out_shape=[jax.ShapeDtypeStruct((T, D), F32), jax.ShapeDtypeStruct((T, NQ), BF16)],
        compiler_params=_cparams(("parallel",)),
        name="post_attention",
    )(x2d, ya, yb, ga, gb, wd, wl, wo, g1, b1, wq)


def _top_rows(s, k, payload=None):
    n = s.shape[0]
    row = lax.broadcasted_iota(I32, s.shape, 0)
    vals, picks = [], []
    for _ in range(k):
        m = jnp.max(s, axis=0, keepdims=True)
        idx = jnp.min(jnp.where(s == m, row, n), axis=0, keepdims=True)
        hit = row == idx
        if payload is not None:
            idx = jnp.max(jnp.where(hit, payload, -1), axis=0, keepdims=True)
        s = jnp.where(hit, -jnp.inf, s)
        vals.append(m)
        picks.append(idx)
    return jnp.concatenate(vals, axis=0), jnp.concatenate(picks, axis=0)


def _candidate_ranks():
    return [(a, PEER_TOPK // (a + 1)) for a in range(PEER_TOPK)]


def _retrieve_kernel(pq_ref, keys_ref, row_ref, sh_ref, g_ref):
    e_rows, g_rows = [], []
    for h in range(PEER_HEADS):
        tops = []
        for c in range(2):
            col = (h * 2 + c) * PEER_DKEY
            qh = pq_ref[:, col:col + PEER_DKEY]
            s = lax.dot_general(keys_ref[h, c], qh, (((1,), (1,)), ((), ())),
                                preferred_element_type=F32)
            tops.append(_top_rows(s, PEER_TOPK))
        (s0, i0), (s1, i1) = tops
        ranks = _candidate_ranks()
        n_cand = sum(nb for _, nb in ranks)
        pad = -n_cand % SUBLANES
        cand = jnp.concatenate([s0[a:a + 1] + s1[:nb] for a, nb in ranks]
                               + [jnp.full((pad, s0.shape[1]), -jnp.inf, F32)], axis=0)
        cand_idx = jnp.concatenate([i0[a:a + 1] * PEER_NKEYS + i1[:nb] for a, nb in ranks]
                                   + [jnp.zeros((pad, s0.shape[1]), I32)], axis=0)
        best, eidx = _top_rows(cand, PEER_TOPK, payload=cand_idx)
        e = jnp.exp(best - jnp.max(best, axis=0, keepdims=True))
        g_rows.append(e / jnp.sum(e, axis=0, keepdims=True))
        e_rows.append(eidx)
    eidx = jnp.concatenate(e_rows, axis=0)
    row_ref[...] = ((eidx >> 1) * SUBLANES).T
    sh_ref[...] = (((eidx & 1) ^ 1) << 4).T
    g_ref[...] = jnp.concatenate(g_rows, axis=0).T


def _peer_retrieve(pq, keys_bf16, tt=128):
    T = pq.shape[0]
    out = pl.BlockSpec((tt, PEER_PAIRS), lambda i: (i, 0))
    return pl.pallas_call(
        _retrieve_kernel,
        grid=(T // tt,),
        in_specs=[pl.BlockSpec((tt, pq.shape[1]), lambda i: (i, 0)),
                  pl.BlockSpec(keys_bf16.shape, lambda i: (0, 0, 0, 0))],
        out_specs=[out, out, out],
        out_shape=[jax.ShapeDtypeStruct((T, PEER_PAIRS), I32), jax.ShapeDtypeStruct((T, PEER_PAIRS), I32),
                   jax.ShapeDtypeStruct((T, PEER_PAIRS), F32)],
        compiler_params=_cparams(("parallel",)),
        name="peer_retrieve",
    )(pq, keys_bf16)


def _pack_kernel(w_ref, o_ref):
    n_tiles = o_ref.shape[0] // SUBLANES
    d = SUBLANES * LANES

    def bf16_bits_high(x):
        return pltpu.bitcast(x.astype(BF16).astype(F32), I32)

    word = lax.shift_right_logical(bf16_bits_high(w_ref[:, :d]), 16) | bf16_bits_high(w_ref[:, d:])
    for c in range(SUBLANES):
        o_ref[pl.ds(c, n_tiles, stride=SUBLANES), :] = word[:, c * LANES:(c + 1) * LANES]


def _pack_table(w_layers, layer, tiles_per_step=256):
    n_layers, n, d = w_layers.shape
    assert d == SUBLANES * LANES
    n_steps = n // (2 * tiles_per_step)
    return pl.pallas_call(
        _pack_kernel,
        grid=(n_steps,),
        in_specs=[pl.BlockSpec((tiles_per_step, 2 * d), lambda i: (layer * n_steps + i, 0))],
        out_specs=pl.BlockSpec((tiles_per_step * SUBLANES, LANES), lambda i: (i, 0)),
        out_shape=jax.ShapeDtypeStruct((n // 2 * SUBLANES, LANES), I32),
        compiler_params=_cparams(("parallel",)),
        name="pack_table",
    )(w_layers.reshape(n_layers * n // 2, 2 * d))


def _expert_tile(tab_ref, row, shift):
    tile = tab_ref[pl.ds(pl.multiple_of(row, SUBLANES), SUBLANES), :]
    return pltpu.bitcast((tile << shift) & HI16, F32)


def _token_tile(ref, t):
    sub = lax.broadcasted_iota(I32, (SUBLANES, LANES), 0)
    full = ref[pl.ds(t, 1), :]
    tile = None
    for c in range(SUBLANES):
        row = jnp.broadcast_to(full[:, c * LANES:(c + 1) * LANES], (SUBLANES, LANES))
        tile = row if tile is None else jnp.where(sub == c, row, tile)
    return tile


def _lane_rows(ref, t):
    return jnp.broadcast_to(ref[pl.ds(t, 1), :], (PEER_PAIRS, LANES)).T


def _staged_token_loop(rows_ref, idx_ref, sems, tt, prep, compute, finish):
    n_chunks = tt // STAGE_TOKENS

    def stage(chunk, slot):
        src = rows_ref.at[pl.ds(pl.multiple_of(chunk * STAGE_TOKENS, STAGE_TOKENS), STAGE_TOKENS), :]
        return pltpu.make_async_copy(src, idx_ref.at[slot], sems.at[slot])

    stage(0, 0).start()
    stage(1, 1).start()
    prep(0, 0)

    def body(i, carry):
        for slot in range(2):
            chunk = 2 * i + slot
            stage(chunk, slot).wait()
            for g in range(STAGE_TOKENS):
                t = chunk * STAGE_TOKENS + g
                vslot = g % 2
                prep(jnp.minimum(t + 1, tt - 1), 1 - vslot)
                finish(jnp.maximum(t - 1, 0), 1 - vslot)
                compute(t, vslot, functools.partial(lambda s, gg, k: idx_ref[s, gg, k], slot, g))
            stage(jnp.minimum(chunk + 2, n_chunks - 1), slot).start()
        return carry

    lax.fori_loop(0, n_chunks // 2, body, 0)
    stage(n_chunks - 1, 0).wait()
    stage(n_chunks - 1, 1).wait()
    finish(tt - 1, (tt - 1) % 2)


def _peer_u_kernel(row_ref, sh_ref, x_ref, tab_ref, h_ref, shb_ref, prod_ref, part_ref, idx_ref, sems):
    tt = x_ref.shape[0]
    tile_rows = SUBLANES * SUBLANES

    def prep(t, slot):
        shb_ref[slot] = _lane_rows(sh_ref, t)

    def compute(t, slot, row_of):
        x = _token_tile(x_ref, t)
        for g in range(PEER_PAIRS // SUBLANES):
            base = g * tile_rows
            for j in range(SUBLANES):
                k = g * SUBLANES + j
                f = _expert_tile(tab_ref, row_of(k), shb_ref[slot, k:k + 1, :])
                pos = j // 2 + (SUBLANES // 2) * (j % 2)
                prod_ref[slot, base + pos * SUBLANES:base + (pos + 1) * SUBLANES, :] = f * x
            quads = []
            for half in range(2):
                hb = base + half * (tile_rows // 2)
                r4 = prod_ref[slot, pl.ds(hb, SUBLANES, stride=HALF_SUBLANES), :]
                for c in range(1, HALF_SUBLANES):
                    r4 = r4 + prod_ref[slot, pl.ds(hb + c, SUBLANES, stride=HALF_SUBLANES), :]
                quads.append(r4 + pltpu.roll(r4, SUBLANES - 1, axis=0))
            even = lax.broadcasted_iota(I32, (SUBLANES, LANES), 0) % 2 == 0
            part_ref[slot, g * SUBLANES:(g + 1) * SUBLANES, :] = jnp.where(
                even, quads[0], pltpu.roll(quads[1], 1, axis=0))

    def finish(t, slot):
        h_ref[pl.ds(t, 1), :] = jnp.sum(part_ref[slot].T, axis=0, keepdims=True)

    part_ref[1] = jnp.zeros(part_ref.shape[1:], F32)
    _staged_token_loop(row_ref, idx_ref, sems, tt, prep, compute, finish)


def _stage_scratch():
    return [pltpu.SMEM((2, STAGE_TOKENS, PEER_PAIRS), I32), pltpu.SemaphoreType.DMA((2,))]


def _peer_u(rows, shifts, x1, utab, tt=128):
    T, D = x1.shape
    pair_spec = pl.BlockSpec((tt, PEER_PAIRS), lambda i: (i, 0))
    return pl.pallas_call(
        _peer_u_kernel,
        grid=(T // tt,),
        in_specs=[pair_spec, pair_spec,
                  pl.BlockSpec((tt, D), lambda i: (i, 0)),
                  pl.BlockSpec(memory_space=pltpu.VMEM)],
        out_specs=pair_spec,
        out_shape=jax.ShapeDtypeStruct((T, PEER_PAIRS), F32),
        scratch_shapes=[pltpu.VMEM((2, PEER_PAIRS, LANES), I32),
                        pltpu.VMEM((2, PEER_PAIRS * SUBLANES, LANES), F32),
                        pltpu.VMEM((2, PEER_PAIRS, LANES), F32)] + _stage_scratch(),
        compiler_params=_cparams(("arbitrary",)),
        name="peer_u",
    )(rows, shifts, x1, utab)


def _peer_v_kernel(row_ref, sh_ref, g_ref, h_ref, tab_ref, o_ref, w_ref, shb_ref, wb_ref, idx_ref, sems):
    tt = o_ref.shape[0]
    n_acc = 4
    h = h_ref[...]
    w_ref[...] = g_ref[...] * (0.5 * h * (1.0 + lax.erf(h * (2.0 ** -0.5))))

    def prep(t, slot):
        shb_ref[slot] = _lane_rows(sh_ref, t)
        wb_ref[slot] = _lane_rows(w_ref, t)

    def compute(t, slot, row_of):
        accs = [None] * n_acc
        for k in range(PEER_PAIRS):
            f = _expert_tile(tab_ref, row_of(k), shb_ref[slot, k:k + 1, :])
            term = wb_ref[slot, k:k + 1, :] * f
            accs[k % n_acc] = term if accs[k % n_acc] is None else accs[k % n_acc] + term
        o_ref[t] = functools.reduce(lambda a, b: a + b, accs)

    def finish(t, slot):
        pass

    _staged_token_loop(row_ref, idx_ref, sems, tt, prep, compute, finish)


def _peer_v(rows, shifts, gates, h, vtab, tt=128):
    T = rows.shape[0]
    pair_spec = pl.BlockSpec((tt, PEER_PAIRS), lambda i: (i, 0))
    return pl.pallas_call(
        _peer_v_kernel,
        grid=(T // tt,),
        in_specs=[pair_spec, pair_spec, pair_spec, pair_spec,
                  pl.BlockSpec(memory_space=pltpu.VMEM)],
        out_specs=pl.BlockSpec((tt, SUBLANES, LANES), lambda i: (i, 0, 0)),
        out_shape=jax.ShapeDtypeStruct((T, SUBLANES, LANES), F32),
        scratch_shapes=[pltpu.VMEM((tt, PEER_PAIRS), F32),
                        pltpu.VMEM((2, PEER_PAIRS, LANES), I32),
                        pltpu.VMEM((2, PEER_PAIRS, LANES), F32)] + _stage_scratch(),
        compiler_params=_cparams(("arbitrary",)),
        name="peer_v",
    )(rows, shifts, gates, h, vtab)


def _final_kernel(x1_ref, f_ref, g_ref, b_ref, o_ref):
    o_ref[...] = _layer_norm(ALPHA * x1_ref[...] + f_ref[...], g_ref[...], b_ref[...])


def _final_norm(x1, ffn, g, b, tm=512):
    T, D = x1.shape
    return pl.pallas_call(
        _final_kernel,
        grid=(T // tm,),
        in_specs=[pl.BlockSpec((tm, D), lambda i: (i, 0)), pl.BlockSpec((tm, D), lambda i: (i, 0)),
                  pl.BlockSpec((1, D), lambda i: (0, 0)), pl.BlockSpec((1, D), lambda i: (0, 0))],
        out_specs=pl.BlockSpec((tm, D), lambda i: (i, 0)),
        out_shape=jax.ShapeDtypeStruct((T, D), F32),
        compiler_params=_cparams(("parallel",)),
        name="final_norm",
    )(x1, ffn, g, b)


def kernel(x, w_in, diff_lambda, diff_subln_g, w_branch_diff, w_branch_dil, w_out, ln1_g, ln1_b,
           peer_wq, peer_keys, peer_u, peer_v, ln2_g, ln2_b, rel_bias):
    B, S, D = x.shape
    T = B * S
    assert w_in.shape[0] == DEPTH
    x2d = x.reshape(T, D)
    for l in range(DEPTH):
        lam_init = 0.8 - 0.6 * math.exp(-0.3 * l)
        lp = diff_lambda[l].astype(F32)
        lam = jnp.exp(jnp.sum(lp[0] * lp[1])) - jnp.exp(jnp.sum(lp[2] * lp[3])) + lam_init

        dq, dk, dv, lq, lk, lv, ga, gb = _input_projection(x2d, w_in[l].astype(BF16))
        ya = _diff_attention(dq, dk, dv, lam, lam_init, rel_bias[:, :DIFF_HEADS], diff_subln_g[l], B, S)
        yb = _dilated_attention(lq, lk, lv, rel_bias[:, DIFF_HEADS:], B, S)
        x1, pq = _post_attention(
            x2d, ya, yb, ga, gb, w_branch_diff[l].astype(BF16), w_branch_dil[l].astype(BF16),
            w_out[l].astype(BF16), ln1_g[l].reshape(1, D), ln1_b[l].reshape(1, D), peer_wq[l].astype(BF16))

        rows, shifts, gates = _peer_retrieve(pq, peer_keys[l].astype(BF16))
        h = _peer_u(rows, shifts, x1, _pack_table(peer_u, l))
        ffn = _peer_v(rows, shifts, gates, h, _pack_table(peer_v, l))
        x2d = _final_norm(x1, ffn.reshape(T, D), ln2_g[l].reshape(1, D), ln2_b[l].reshape(1, D))
    return x2d.reshape(B, S, D)
```
